```python
import jax
import jax.numpy as jnp
from jax import lax
import numpy as np

D_MODEL = 1024
BATCH = 8
SEQ = 8192
DEPTH = 2

HEAD_DIM = 64
GRID_W = 64
BLOCK = 128
ROPE_THETA = 10000.0
EPS = 1e-6
NEG_INF = -1e30
A_HEADS = 6
A_KV_HEADS = 2
B_HEADS = 6
B_KV_HEADS = 2
WINDOW = 128
C_HEADS = 4
C_Q_LORA = 256
C_KV_LORA = 128
C_NOPE = 64
C_ROPE = 32
C_V = 64
A_WIDTH = A_HEADS * HEAD_DIM
B_WIDTH = B_HEADS * HEAD_DIM
C_WIDTH = C_HEADS * C_V
MIX = A_WIDTH + B_WIDTH + C_WIDTH
A_IN = (A_HEADS + 2 * A_KV_HEADS) * HEAD_DIM
B_IN = (B_HEADS + 2 * B_KV_HEADS) * HEAD_DIM
C_IN = C_Q_LORA + C_KV_LORA + C_ROPE
IN_W = A_IN + B_IN + C_IN
N_EXPERTS = 256
TOP_K = 8
N_GROUPS = 8
TOPK_GROUPS = 4
EXPERT_FF = 256
SHARED_FF = 256
ROUTED_SCALE = 2.5
MOE_BLOCK = 128

kernel_name = 'hybrid_parallel_heads_mla_swa_axial_moe_encoder'


def rms_norm(x, g):
    xf = x.astype(jnp.float32)
    y = xf * lax.rsqrt(jnp.mean(xf * xf, axis=-1, keepdims=True) + EPS)
    return (y * g.astype(jnp.float32)).astype(x.dtype)


def rope_cos_sin(pos, dim):
    inv = ROPE_THETA ** (-jnp.arange(0, dim, 2, dtype=jnp.float32) / dim)
    ang = pos.astype(jnp.float32)[:, None] * inv[None, :]
    return jnp.cos(ang), jnp.sin(ang)


def apply_rope(x, cos, sin):
    half = x.shape[-1] // 2
    shp = (cos.shape[0],) + (1,) * (x.ndim - 3) + (half,)
    cos = cos.reshape(shp).astype(x.dtype)
    sin = sin.reshape(shp).astype(x.dtype)
    x1, x2 = x[..., :half], x[..., half:]
    return jnp.concatenate([x1 * cos - x2 * sin, x1 * sin + x2 * cos], axis=-1)


def axial_rope(x, row_ids, col_ids):
    half = HEAD_DIM // 2
    cr, sr = rope_cos_sin(row_ids, half)
    cc, sc = rope_cos_sin(col_ids, half)
    return jnp.concatenate([apply_rope(x[..., :half], cr, sr), apply_rope(x[..., half:], cc, sc)], axis=-1)


def mixer_a(p, row_ids, col_ids, q_g, k_g):
    bsz, seq, _ = p.shape
    grp = A_HEADS // A_KV_HEADS
    kv_w = A_KV_HEADS * HEAD_DIM
    q = p[..., :A_WIDTH].reshape(bsz, seq, A_KV_HEADS, grp, HEAD_DIM)
    k = p[..., A_WIDTH:A_WIDTH + kv_w].reshape(bsz, seq, A_KV_HEADS, HEAD_DIM)
    v = p[..., A_WIDTH + kv_w:].reshape(bsz, seq, A_KV_HEADS, HEAD_DIM)
    q = axial_rope(rms_norm(q, q_g), row_ids, col_ids)
    k = axial_rope(rms_norm(k, k_g), row_ids, col_ids)
    scale = HEAD_DIM ** -0.5
    nq = seq // BLOCK
    qb = jnp.moveaxis(q.reshape(bsz, nq, BLOCK, A_KV_HEADS, grp, HEAD_DIM), 1, 0)

    def attend(qi):
        s = jnp.einsum('bqhgd,bkhd->bhgqk', qi, k, preferred_element_type=jnp.float32) * scale
        pr = jax.nn.softmax(s, axis=-1).astype(v.dtype)
        return jnp.einsum('bhgqk,bkhd->bqhgd', pr, v)

    o = lax.map(attend, qb)
    return jnp.moveaxis(o, 0, 1).reshape(bsz, seq, A_WIDTH)


def mixer_b(p, sink):
    bsz, seq, _ = p.shape
    grp = B_HEADS // B_KV_HEADS
    kv_w = B_KV_HEADS * HEAD_DIM
    nb = seq // BLOCK
    q = p[..., :B_WIDTH].reshape(bsz, nb, BLOCK, B_KV_HEADS, grp, HEAD_DIM)
    k = p[..., B_WIDTH:B_WIDTH + kv_w].reshape(bsz, seq, B_KV_HEADS, HEAD_DIM)
    v = p[..., B_WIDTH + kv_w:].reshape(bsz, seq, B_KV_HEADS, HEAD_DIM)

    def band(t):
        tp = jnp.pad(t, ((0, 0), (BLOCK, BLOCK), (0, 0), (0, 0))).reshape(bsz, nb + 2, BLOCK, B_KV_HEADS, HEAD_DIM)
        return jnp.concatenate([tp[:, :-2], tp[:, 1:-1], tp[:, 2:]], axis=2)

    kw, vw = band(k), band(v)
    s = jnp.einsum('bnqhgd,bnkhd->bnhgqk', q, kw, preferred_element_type=jnp.float32) * HEAD_DIM ** -0.5
    rel = jnp.arange(3 * BLOCK)[None, :] - BLOCK - jnp.arange(BLOCK)[:, None]
    dist = jnp.abs(rel).astype(jnp.float32)
    key_pos = jnp.arange(nb)[:, None] * BLOCK - BLOCK + jnp.arange(3 * BLOCK)[None, :]
    in_range = (key_pos >= 0) & (key_pos < seq)
    valid = (jnp.abs(rel) <= WINDOW)[None, :, :] & in_range[:, None, :]
    slopes = jnp.exp2(-8.0 * jnp.arange(1, B_HEADS + 1, dtype=jnp.float32) / B_HEADS).reshape(B_KV_HEADS, grp)
    s = s - slopes[:, :, None, None] * dist
    s = jnp.where(valid[None, :, None, None], s, NEG_INF)
    sk = sink.astype(jnp.float32).reshape(B_KV_HEADS, grp)[None, None, :, :, None, None]
    m = jnp.maximum(jnp.max(s, axis=-1, keepdims=True), sk)
    e = jnp.exp(s - m)
    pr = (e / (jnp.sum(e, axis=-1, keepdims=True) + jnp.exp(sk - m))).astype(vw.dtype)
    o = jnp.einsum('bnhgqk,bnkhd->bnqhgd', pr, vw)
    return o.reshape(bsz, seq, B_WIDTH)


def mixer_c(p, pos, qn_g, kvn_g, w_uq, w_uk, w_uv):
    bsz, seq, _ = p.shape
    cq = rms_norm(p[..., :C_Q_LORA], qn_g)
    ckv = rms_norm(p[..., C_Q_LORA:C_Q_LORA + C_KV_LORA], kvn_g)
    kr = p[..., C_Q_LORA + C_KV_LORA:]
    q = jnp.einsum('bsr,re->bse', cq, w_uq).reshape(bsz, seq, C_HEADS, C_NOPE + C_ROPE)
    k_nope = jnp.einsum('bsr,re->bse', ckv, w_uk).reshape(bsz, seq, C_HEADS, C_NOPE)
    v = jnp.einsum('bsr,re->bse', ckv, w_uv).reshape(bsz, seq, C_HEADS, C_V)
    cos, sin = rope_cos_sin(pos, C_ROPE)
    q_nope = q[..., :C_NOPE]
    q_rope = apply_rope(q[..., C_NOPE:], cos, sin)
    k_rope = apply_rope(kr, cos, sin)
    scale = (C_NOPE + C_ROPE) ** -0.5
    nq = seq // BLOCK
    qnb = jnp.moveaxis(q_nope.reshape(bsz, nq, BLOCK, C_HEADS, C_NOPE), 1, 0)
    qrb = jnp.moveaxis(q_rope.reshape(bsz, nq, BLOCK, C_HEADS, C_ROPE), 1, 0)

    def attend(blk):
        qn, qr = blk
        s = (jnp.einsum('bqhd,bkhd->bhqk', qn, k_nope, preferred_element_type=jnp.float32)
             + jnp.einsum('bqhr,bkr->bhqk', qr, k_rope, preferred_element_type=jnp.float32)) * scale
        pr = jax.nn.softmax(s, axis=-1).astype(v.dtype)
        return jnp.einsum('bhqk,bkhd->bqhd', pr, v)

    o = lax.map(attend, (qnb, qrb))
    return jnp.moveaxis(o, 0, 1).reshape(bsz, seq, C_WIDTH)


def moe_ffn(h, router_w, router_b, w1, w3, w2, sw1, sw3, sw2):
    bsz, seq, d = h.shape
    n_tok = bsz * seq
    xt = h.reshape(n_tok, d)
    scores = jax.nn.sigmoid(jnp.einsum('nd,de->ne', xt, router_w, preferred_element_type=jnp.float32))
    choice = scores + router_b.astype(jnp.float32)
    per_grp = N_EXPERTS // N_GROUPS
    grp_score = jnp.sum(lax.top_k(choice.reshape(n_tok, N_GROUPS, per_grp), 2)[0], axis=-1)
    grp_idx = lax.top_k(grp_score, TOPK_GROUPS)[1]
    grp_keep = jnp.any(grp_idx[:, :, None] == jnp.arange(N_GROUPS)[None, None, :], axis=1)
    masked = jnp.where(jnp.repeat(grp_keep, per_grp, axis=1), choice, NEG_INF)
    top_idx = lax.top_k(masked, TOP_K)[1]
    top_w = jnp.take_along_axis(scores, top_idx, axis=-1)
    top_w = top_w / jnp.sum(top_w, axis=-1, keepdims=True) * ROUTED_SCALE
    n_assign = n_tok * TOP_K
    e_flat = top_idx.reshape(n_assign).astype(jnp.int32)
    tok_flat = jnp.repeat(jnp.arange(n_tok, dtype=jnp.int32), TOP_K)
    w_flat = top_w.reshape(n_assign)
    order = jnp.argsort(e_flat)
    e_s, tok_s, w_s = e_flat[order], tok_flat[order], w_flat[order]
    counts = jnp.bincount(e_flat, length=N_EXPERTS).astype(jnp.int32)
    starts = jnp.cumsum(counts) - counts
    padded = (counts + MOE_BLOCK - 1) // MOE_BLOCK * MOE_BLOCK
    pends = jnp.cumsum(padded)
    pstarts = pends - padded
    dest = pstarts[e_s] + jnp.arange(n_assign, dtype=jnp.int32) - starts[e_s]
    n_blocks = -(-n_assign // MOE_BLOCK) + N_EXPERTS
    buf_len = n_blocks * MOE_BLOCK
    buf_tok = jnp.zeros((buf_len,), jnp.int32).at[dest].set(tok_s)
    buf_w = jnp.zeros((buf_len,), jnp.float32).at[dest].set(w_s)
    block_start = jnp.arange(n_blocks, dtype=jnp.int32) * MOE_BLOCK
    block_exp = jnp.minimum(jnp.searchsorted(pends, block_start, side='right'), N_EXPERTS - 1).astype(jnp.int32)

    def step(acc, blk):
        tok, wt, e = blk
        xb = xt[tok]
        hid = jax.nn.silu(xb @ w1[e]) * (xb @ w3[e])
        y = (hid @ w2[e]).astype(jnp.float32) * wt[:, None]
        return acc.at[tok].add(y), None

    routed, _ = lax.scan(step, jnp.zeros((n_tok, d), jnp.float32),
                         (buf_tok.reshape(n_blocks, MOE_BLOCK), buf_w.reshape(n_blocks, MOE_BLOCK), block_exp))
    shared = (jax.nn.silu(xt @ sw1) * (xt @ sw3)) @ sw2
    return (routed.astype(h.dtype) + shared).reshape(bsz, seq, d)


def setup_inputs(seed: int = 0) -> dict:
    key = jax.random.key(seed)
    ks = jax.random.split(key, 26)

    def nrm(k, shape, scale):
        return jax.random.normal(k, shape, jnp.float32) * scale

    def gain(k, shape):
        return 1.0 + 0.05 * jax.random.normal(k, shape, jnp.float32)

    L, D, E, F = DEPTH, D_MODEL, N_EXPERTS, EXPERT_FF
    return {
        'x': nrm(ks[0], (BATCH, SEQ, D), 1.0),
        'c': nrm(ks[1], (BATCH, D), 1.0),
        'mod_w': nrm(ks[2], (L, D, 6 * D), 0.5 * D ** -0.5),
        'mod_b': nrm(ks[3], (L, 6 * D), 0.02),
        'norm1_g': gain(ks[4], (L, D)),
        'norm2_g': gain(ks[5], (L, D)),
        'w_in': nrm(ks[6], (L, D, IN_W), D ** -0.5),
        'a_qnorm_g': gain(ks[7], (L, HEAD_DIM)),
        'a_knorm_g': gain(ks[8], (L, HEAD_DIM)),
        'b_sink': nrm(ks[9], (L, B_HEADS), 0.5),
        'c_qnorm_g': gain(ks[10], (L, C_Q_LORA)),
        'c_kvnorm_g': gain(ks[11], (L, C_KV_LORA)),
        'c_w_uq': nrm(ks[12], (L, C_Q_LORA, C_HEADS * (C_NOPE + C_ROPE)), C_Q_LORA ** -0.5),
        'c_w_uk': nrm(ks[13], (L, C_KV_LORA, C_HEADS * C_NOPE), C_KV_LORA ** -0.5),
        'c_w_uv': nrm(ks[14], (L, C_KV_LORA, C_HEADS * C_V), C_KV_LORA ** -0.5),
        'out_norm_g': gain(ks[15], (L, MIX)),
        'w_out': nrm(ks[16], (L, MIX, D), MIX ** -0.5),
        'router_w': nrm(ks[17], (L, D, E), D ** -0.5),
        'router_b': nrm(ks[18], (L, E), 0.01),
        'exp_w1': nrm(ks[19], (L, E, D, F), D ** -0.5),
        'exp_w3': nrm(ks[20], (L, E, D, F), D ** -0.5),
        'exp_w2': nrm(ks[21], (L, E, F, D), F ** -0.5),
        'sh_w1': nrm(ks[22], (L, D, SHARED_FF), D ** -0.5),
        'sh_w3': nrm(ks[23], (L, D, SHARED_FF), D ** -0.5),
        'sh_w2': nrm(ks[24], (L, SHARED_FF, D), SHARED_FF ** -0.5),
        'final_g': gain(ks[25], (D,)),
    }


def reference(x, c, mod_w, mod_b, norm1_g, norm2_g, w_in, a_qnorm_g, a_knorm_g, b_sink,
              c_qnorm_g, c_kvnorm_g, c_w_uq, c_w_uk, c_w_uv, out_norm_g, w_out,
              router_w, router_b, exp_w1, exp_w3, exp_w2, sh_w1, sh_w3, sh_w2, final_g):
    seq = x.shape[1]
    rows = seq // GRID_W
    row_ids = jnp.repeat(jnp.arange(rows, dtype=jnp.int32), GRID_W)
    col_ids = jnp.tile(jnp.arange(GRID_W, dtype=jnp.int32), rows)
    pos = jnp.arange(seq, dtype=jnp.int32)
    c_act = jax.nn.silu(c)
    for l in range(DEPTH):
        mod = jnp.einsum('bd,de->be', c_act, mod_w[l]) + mod_b[l]
        sh1, sc1, g1, sh2, sc2, g2 = [m[:, None, :] for m in jnp.split(mod, 6, axis=-1)]
        u = rms_norm(x, norm1_g[l]) * (1 + sc1) + sh1
        proj = jnp.einsum('bsd,de->bse', u, w_in[l])
        oa = mixer_a(proj[..., :A_IN], row_ids, col_ids, a_qnorm_g[l], a_knorm_g[l])
        ob = mixer_b(proj[..., A_IN:A_IN + B_IN], b_sink[l])
        oc = mixer_c(proj[..., A_IN + B_IN:], pos, c_qnorm_g[l], c_kvnorm_g[l], c_w_uq[l], c_w_uk[l], c_w_uv[l])
        g = out_norm_g[l]
        mixed = jnp.concatenate([rms_norm(oa, g[:A_WIDTH]),
                                 rms_norm(ob, g[A_WIDTH:A_WIDTH + B_WIDTH]),
                                 rms_norm(oc, g[A_WIDTH + B_WIDTH:])], axis=-1)
        x = x + g1 * jnp.einsum('bse,ed->bsd', mixed, w_out[l])
        u = rms_norm(x, norm2_g[l]) * (1 + sc2) + sh2
        x = x + g2 * moe_ffn(u, router_w[l], router_b[l], exp_w1[l], exp_w3[l], exp_w2[l],
                             sh_w1[l], sh_w3[l], sh_w2[l])
    return rms_norm(x, final_g)
```

```python
import functools

import jax
import jax.numpy as jnp
from jax import lax
from jax.experimental import pallas as pl
from jax.experimental.pallas import tpu as pltpu

F32 = jnp.float32
BF16 = jnp.bfloat16

HEAD_DIM = 64
GRID_W = 64
ROPE_THETA = 10000.0
EPS = 1e-6
NEG_INF = -1e30
A_HEADS, A_KV_HEADS = 6, 2
B_HEADS, B_KV_HEADS = 6, 2
WINDOW = 128
C_HEADS, C_Q_LORA, C_KV_LORA, C_NOPE, C_ROPE, C_V = 4, 256, 128, 64, 32, 64
A_WIDTH = A_HEADS * HEAD_DIM
B_WIDTH = B_HEADS * HEAD_DIM
C_WIDTH = C_HEADS * C_V
A_IN = (A_HEADS + 2 * A_KV_HEADS) * HEAD_DIM
B_IN = (B_HEADS + 2 * B_KV_HEADS) * HEAD_DIM
TOP_K = 8
N_GROUPS = 8
TOPK_GROUPS = 4
ROUTED_SCALE = 2.5
LOG2E = 1.4426950408889634

LANES = 128
OFF_AQ = 0
OFF_AK = OFF_AQ + A_HEADS * LANES
OFF_AV = OFF_AK + LANES
OFF_BQ = OFF_AV + LANES
OFF_BK = OFF_BQ + B_HEADS * LANES
OFF_BV = OFF_BK + LANES
OFF_CQ = OFF_BV + LANES
OFF_CKV = OFF_CQ + C_Q_LORA
OFF_CKR = OFF_CKV + C_KV_LORA
IN_PAD = OFF_CKR + LANES

MOE_BM = 256
VMEM_LIMIT = 56 * 1024 * 1024


def _tile(n, pref):
    t = min(n, pref)
    assert n % t == 0, (n, t)
    return t


def _cparams(sem, vmem=VMEM_LIMIT):
    return pltpu.CompilerParams(dimension_semantics=sem, vmem_limit_bytes=vmem)


def _mod_kernel(c_ref, w_ref, b_ref, o_ref):
    c = c_ref[...]
    ca = c * (1.0 / (1.0 + jnp.exp(-c)))
    o_ref[0] = jnp.dot(ca, w_ref[0], preferred_element_type=F32, precision=lax.Precision.HIGHEST) + b_ref[0]


def _mod(c, mod_w, mod_b):
    L, D, W = mod_w.shape
    B = c.shape[0]
    tn = _tile(W, 1536)
    return pl.pallas_call(
        _mod_kernel,
        out_shape=jax.ShapeDtypeStruct((L, B, W), F32),
        grid=(L, W // tn),
        in_specs=[
            pl.BlockSpec((B, D), lambda l, j: (0, 0)),
            pl.BlockSpec((1, D, tn), lambda l, j: (l, 0, j)),
            pl.BlockSpec((1, 1, tn), lambda l, j: (l, 0, j)),
        ],
        out_specs=pl.BlockSpec((1, B, tn), lambda l, j: (l, 0, j)),
        compiler_params=_cparams(("arbitrary", "arbitrary")),
        name="mod_proj",
    )(c, mod_w, mod_b.reshape(L, 1, W))


def _rope(v, c, sa, sb):
    return v * c + pltpu.roll(v, LANES - 16, 1) * sa + pltpu.roll(v, 16, 1) * sb


def _inproj_kernel(x_ref, sc_ref, sh_ref, g_ref, w_ref, ca_ref, saa_ref, sba_ref, cc_ref, sac_ref, sbc_ref,
                   gq_ref, gk_ref, gcq_ref, gckv_ref, wuq_ref, wuk_ref, wuv_ref,
                   qa_ref, ka_ref, va_ref, qb_ref, kb_ref, vb_ref, qc_ref, kc_ref, vc_ref, *, qs_ab, qs_c):
    x = x_ref[0]
    u = x * lax.rsqrt(jnp.mean(x * x, axis=-1, keepdims=True) + EPS) * g_ref[...]
    u = u * (1.0 + sc_ref[0]) + sh_ref[0]
    p = jnp.dot(u.astype(BF16), w_ref[...], preferred_element_type=F32)
    ca, saa, sba = ca_ref[...], saa_ref[...], sba_ref[...]
    cc, sac, sbc = cc_ref[...], sac_ref[...], sbc_ref[...]

    gq = gq_ref[...]
    for h in range(A_HEADS):
        v = p[:, OFF_AQ + LANES * h:OFF_AQ + LANES * (h + 1)]
        ss = jnp.sum(v * v, axis=-1, keepdims=True) * (1.0 / HEAD_DIM)
        v = v * lax.rsqrt(ss + EPS) * gq
        qa_ref[0, h] = (_rope(v, ca, saa, sba) * qs_ab).astype(BF16)
    k = p[:, OFF_AK:OFF_AK + LANES]
    lane = lax.broadcasted_iota(jnp.int32, k.shape, 1)
    k2 = k * k
    s0 = jnp.sum(jnp.where(lane < HEAD_DIM, k2, 0.0), axis=-1, keepdims=True)
    s1 = jnp.sum(k2, axis=-1, keepdims=True) - s0
    r = jnp.where(lane < HEAD_DIM, lax.rsqrt(s0 * (1.0 / HEAD_DIM) + EPS), lax.rsqrt(s1 * (1.0 / HEAD_DIM) + EPS))
    ka_ref[0] = _rope(k * r * gk_ref[...], ca, saa, sba).astype(BF16)
    va_ref[0] = p[:, OFF_AV:OFF_AV + LANES].astype(BF16)

    for h in range(B_HEADS):
        qb_ref[0, h] = (p[:, OFF_BQ + LANES * h:OFF_BQ + LANES * (h + 1)] * qs_ab).astype(BF16)
    kb_ref[0] = p[:, OFF_BK:OFF_BK + LANES].astype(BF16)
    vb_ref[0] = p[:, OFF_BV:OFF_BV + LANES].astype(BF16)

    cq = p[:, OFF_CQ:OFF_CQ + C_Q_LORA]
    cq = cq * lax.rsqrt(jnp.mean(cq * cq, axis=-1, keepdims=True) + EPS) * gcq_ref[...]
    qh = jnp.dot(cq.astype(BF16), wuq_ref[...], preferred_element_type=F32)
    for h in range(C_HEADS):
        base = 2 * LANES * h
        qc_ref[0, h, :, 0:LANES] = (qh[:, base:base + LANES] * qs_c).astype(BF16)
        qc_ref[0, h, :, LANES:2 * LANES] = (_rope(qh[:, base + LANES:base + 2 * LANES], cc, sac, sbc) * qs_c).astype(BF16)
    ckv = p[:, OFF_CKV:OFF_CKV + C_KV_LORA]
    ckv = (ckv * lax.rsqrt(jnp.mean(ckv * ckv, axis=-1, keepdims=True) + EPS) * gckv_ref[...]).astype(BF16)
    kn = jnp.dot(ckv, wuk_ref[...], preferred_element_type=F32)
    vv = jnp.dot(ckv, wuv_ref[...], preferred_element_type=F32)
    kr = _rope(p[:, OFF_CKR:OFF_CKR + LANES], cc, sac, sbc).astype(BF16)
    for pr in range(C_HEADS // 2):
        kc_ref[0, pr, :, 0:LANES] = kn[:, LANES * pr:LANES * (pr + 1)].astype(BF16)
        kc_ref[0, pr, :, LANES:2 * LANES] = kr
        vc_ref[0, pr] = vv[:, LANES * pr:LANES * (pr + 1)].astype(BF16)


def _inproj(x, sc1, sh1, g, w_pad, tabs_a, tabs_c, gq, gk, gcq, gckv, wuq_pad, wuk, wuv):
    B, S, D = x.shape
    tm = _tile(S, 512)
    full = lambda shp: pl.BlockSpec(shp, lambda b, i: (0,) * len(shp))
    tab = pl.BlockSpec((tm, LANES), lambda b, i: (i, 0))
    mod = pl.BlockSpec((1, 1, D), lambda b, i: (b, 0, 0))
    kern = functools.partial(_inproj_kernel, qs_ab=HEAD_DIM ** -0.5 * LOG2E, qs_c=(C_NOPE + C_ROPE) ** -0.5 * LOG2E)
    outs = (
        jax.ShapeDtypeStruct((B, A_HEADS, S, LANES), BF16), jax.ShapeDtypeStruct((B, S, LANES), BF16),
        jax.ShapeDtypeStruct((B, S, LANES), BF16),
        jax.ShapeDtypeStruct((B, B_HEADS, S, LANES), BF16), jax.ShapeDtypeStruct((B, S, LANES), BF16),
        jax.ShapeDtypeStruct((B, S, LANES), BF16),
        jax.ShapeDtypeStruct((B, C_HEADS, S, 2 * LANES), BF16), jax.ShapeDtypeStruct((B, C_HEADS // 2, S, 2 * LANES), BF16),
        jax.ShapeDtypeStruct((B, C_HEADS // 2, S, LANES), BF16),
    )
    slab = pl.BlockSpec((1, tm, LANES), lambda b, i: (b, i, 0))
    out_specs = (
        pl.BlockSpec((1, A_HEADS, tm, LANES), lambda b, i: (b, 0, i, 0)), slab, slab,
        pl.BlockSpec((1, B_HEADS, tm, LANES), lambda b, i: (b, 0, i, 0)), slab, slab,
        pl.BlockSpec((1, C_HEADS, tm, 2 * LANES), lambda b, i: (b, 0, i, 0)),
        pl.BlockSpec((1, C_HEADS // 2, tm, 2 * LANES), lambda b, i: (b, 0, i, 0)),
        pl.BlockSpec((1, C_HEADS // 2, tm, LANES), lambda b, i: (b, 0, i, 0)),
    )
    return pl.pallas_call(
        kern, out_shape=outs, grid=(B, S // tm),
        in_specs=[pl.BlockSpec((1, tm, D), lambda b, i: (b, i, 0)), mod, mod, full((1, D)), full((D, IN_PAD)),
                  tab, tab, tab, tab, tab, tab,
                  full((1, LANES)), full((1, LANES)), full((1, C_Q_LORA)), full((1, C_KV_LORA)),
                  full(wuq_pad.shape), full(wuk.shape), full(wuv.shape)],
        out_specs=out_specs,
        compiler_params=_cparams(("arbitrary", "arbitrary")),
        name="norm_inproj",
    )(x, sc1, sh1, g, w_pad, *tabs_a, *tabs_c, gq, gk, gcq, gckv, wuq_pad, wuk, wuv)


def _flash_kernel(q_ref, k_ref, v_ref, e_ref, o_ref, m_ref, l_ref, acc_ref, *, tk):
    H, tq, dk = q_ref.shape[2:]
    S = k_ref.shape[2]
    R = H * tq
    q = q_ref[0, 0].reshape(R, dk)
    m_ref[...] = jnp.full(m_ref.shape, NEG_INF, F32)
    l_ref[...] = jnp.zeros(l_ref.shape, F32)
    acc_ref[...] = jnp.zeros(acc_ref.shape, F32)

    def body(j, carry):
        off = pl.multiple_of(j * tk, tk)
        k = k_ref[0, 0, pl.ds(off, tk), :]
        v = v_ref[0, 0, pl.ds(off, tk), :]
        s = lax.dot_general(q, k, (((1,), (1,)), ((), ())), preferred_element_type=F32)
        m_prev = m_ref[...]
        m_new = jnp.maximum(m_prev, jnp.max(s, axis=-1, keepdims=True))
        p = jnp.exp2(s - m_new)
        alpha = jnp.exp2(m_prev - m_new)
        l_ref[...] = alpha * l_ref[...] + jnp.sum(p, axis=-1, keepdims=True)
        acc_ref[...] = alpha * acc_ref[...] + jnp.dot(p.astype(BF16), v, preferred_element_type=F32)
        m_ref[...] = m_new
        return carry

    lax.fori_loop(0, S // tk, body, 0)
    o = (acc_ref[...] / l_ref[...]).astype(BF16).reshape(H, tq, LANES)
    out = jnp.dot(o[0], e_ref[0], preferred_element_type=F32)
    for h in range(1, H):
        out = out + jnp.dot(o[h], e_ref[h], preferred_element_type=F32)
    o_ref[0] = out.astype(BF16)


def _flash(q, k, v, esel, *, tq, tk):
    B, P, H, S, dk = q.shape
    W = esel.shape[2]
    tq = _tile(S, tq)
    tk = _tile(S, tk)
    R = H * tq
    return pl.pallas_call(
        functools.partial(_flash_kernel, tk=tk),
        out_shape=jax.ShapeDtypeStruct((B, S, P * W), BF16),
        grid=(B, P, S // tq),
        in_specs=[pl.BlockSpec((1, 1, H, tq, dk), lambda b, p, i: (b, p, 0, i, 0)),
                  pl.BlockSpec((1, 1, S, dk), lambda b, p, i: (b, p, 0, 0)),
                  pl.BlockSpec((1, 1, S, LANES), lambda b, p, i: (b, p, 0, 0)),
                  pl.BlockSpec((H, LANES, W), lambda b, p, i: (0, 0, 0))],
        out_specs=pl.BlockSpec((1, tq, W), lambda b, p, i: (b, i, p)),
        scratch_shapes=[pltpu.VMEM((R, 1), F32), pltpu.VMEM((R, 1), F32), pltpu.VMEM((R, LANES), F32)],
        compiler_params=_cparams(("arbitrary", "arbitrary", "arbitrary")),
        name="flash_attn",
    )(q, k, v, esel)


def _window_kernel(q_ref, k_ref, v_ref, e_ref, slope_ref, sink_ref, o_ref, *, tq):
    H = q_ref.shape[1]
    S = k_ref.shape[1]
    wk = min(S, tq + 2 * WINDOW)
    i = pl.program_id(1)
    start = pl.multiple_of(jnp.clip(i * tq - WINDOW, 0, S - wk), WINDOW)
    q = q_ref[0].reshape(H * tq, LANES)
    k = k_ref[0, pl.ds(start, wk), :]
    v = v_ref[0, pl.ds(start, wk), :]
    s = lax.dot_general(q, k, (((1,), (1,)), ((), ())), preferred_element_type=F32).reshape(H, tq, wk)
    tpos = i * tq + lax.broadcasted_iota(jnp.int32, (tq, wk), 0)
    spos = start + lax.broadcasted_iota(jnp.int32, (tq, wk), 1)
    dist = jnp.abs(spos - tpos)
    s = s - slope_ref[...] * dist.astype(F32)[None]
    s = jnp.where((dist <= WINDOW)[None], s, NEG_INF)
    sink = sink_ref[...]
    m = jnp.maximum(jnp.max(s, axis=-1, keepdims=True), sink)
    e = jnp.exp2(s - m)
    den = jnp.sum(e, axis=-1, keepdims=True) + jnp.exp2(sink - m)
    o = jnp.dot(e.reshape(H * tq, wk).astype(BF16), v, preferred_element_type=F32).reshape(H, tq, LANES)
    o = (o / den).astype(BF16)
    out = jnp.dot(o[0], e_ref[0], preferred_element_type=F32)
    for h in range(1, H):
        out = out + jnp.dot(o[h], e_ref[h], preferred_element_type=F32)
    o_ref[0] = out.astype(BF16)


def _window(q, k, v, esel, slopes, sink, *, tq):
    B, H, S, _ = q.shape
    W = esel.shape[2]
    tq = _tile(S, tq)
    return pl.pallas_call(
        functools.partial(_window_kernel, tq=tq),
        out_shape=jax.ShapeDtypeStruct((B, S, W), BF16),
        grid=(B, S // tq),
        in_specs=[pl.BlockSpec((1, H, tq, LANES), lambda b, i: (b, 0, i, 0)),
                  pl.BlockSpec((1, S, LANES), lambda b, i: (b, 0, 0)),
                  pl.BlockSpec((1, S, LANES), lambda b, i: (b, 0, 0)),
                  pl.BlockSpec((H, LANES, W), lambda b, i: (0, 0, 0)),
                  pl.BlockSpec((H, 1, 1), lambda b, i: (0, 0, 0)),
                  pl.BlockSpec((H, 1, 1), lambda b, i: (0, 0, 0))],
        out_specs=pl.BlockSpec((1, tq, W), lambda b, i: (b, i, 0)),
        compiler_params=_cparams(("arbitrary", "arbitrary")),
        name="window_attn",
    )(q, k, v, esel, slopes, sink)


def _sigmoid(v):
    return 1.0 / (1.0 + jnp.exp(-v))


def _post_kernel(oa_ref, ob_ref, oc_ref, x_ref, g1_ref, sc2_ref, sh2_ref, g2_ref, og_ref, wout_ref, n2_ref, rwt_ref,
                 sw1_ref, sw3_ref, sw2_ref, xs1_ref, u2_ref, st_ref):
    og = og_ref[...]

    def rn(o_ref, g):
        o = o_ref[0].astype(F32)
        return (o * lax.rsqrt(jnp.mean(o * o, axis=-1, keepdims=True) + EPS) * g).astype(BF16)

    mixed = jnp.concatenate([rn(oa_ref, og[:, :A_WIDTH]), rn(ob_ref, og[:, A_WIDTH:A_WIDTH + B_WIDTH]),
                             rn(oc_ref, og[:, A_WIDTH + B_WIDTH:])], axis=-1)
    x1 = x_ref[0] + g1_ref[0] * jnp.dot(mixed, wout_ref[...], preferred_element_type=F32)
    u2 = x1 * lax.rsqrt(jnp.mean(x1 * x1, axis=-1, keepdims=True) + EPS) * n2_ref[...]
    u2 = u2 * (1.0 + sc2_ref[0]) + sh2_ref[0]
    u2_ref[...] = u2
    ub = u2.astype(BF16)
    logits_t = lax.dot_general(rwt_ref[...], ub, (((1,), (1,)), ((), ())), preferred_element_type=F32)
    st_ref[...] = _sigmoid(logits_t)
    h1 = jnp.dot(ub, sw1_ref[...], preferred_element_type=F32)
    h3 = jnp.dot(ub, sw3_ref[...], preferred_element_type=F32)
    hid = (h1 * _sigmoid(h1) * h3).astype(BF16)
    xs1_ref[0] = x1 + g2_ref[0] * jnp.dot(hid, sw2_ref[...], preferred_element_type=F32)


def _post(oa, ob, oc, x, g1, sc2, sh2, g2, og, wout, n2, rwt, sw1, sw3, sw2):
    B, S, D = x.shape
    E = rwt.shape[0]
    tm = _tile(S, 512)
    nt = S // tm
    full = lambda a: pl.BlockSpec(a.shape, lambda b, i: (0,) * a.ndim)
    mod = pl.BlockSpec((1, 1, D), lambda b, i: (b, 0, 0))
    row = lambda w: pl.BlockSpec((1, tm, w), lambda b, i: (b, i, 0))
    return pl.pallas_call(
        _post_kernel,
        out_shape=(jax.ShapeDtypeStruct((B, S, D), F32), jax.ShapeDtypeStruct((B * S, D), F32),
                   jax.ShapeDtypeStruct((E, B * S), F32)),
        grid=(B, nt),
        in_specs=[row(A_WIDTH), row(B_WIDTH), row(C_WIDTH), row(D), mod, mod, mod, mod, full(og), full(wout), full(n2),
                  full(rwt), full(sw1), full(sw3), full(sw2)],
        out_specs=(row(D), pl.BlockSpec((tm, D), lambda b, i: (b * nt + i, 0)),
                   pl.BlockSpec((E, tm), lambda b, i: (0, b * nt + i))),
        compiler_params=_cparams(("arbitrary", "arbitrary")),
        name="post_attn",
    )(oa, ob, oc, x, g1, sc2, sh2, g2, og, wout, n2, rwt, sw1, sw3, sw2)


def _route_kernel(st_ref, rb_ref, tri_ref, idx_ref, w_ref, rank_ref, cnt_ref, carry_ref):
    E, tm = st_ref.shape
    per = E // N_GROUPS
    big = float(E)

    @pl.when(pl.program_id(0) == 0)
    def _():
        carry_ref[...] = jnp.zeros(carry_ref.shape, F32)

    s = st_ref[...]
    choice = s + rb_ref[...]
    c3 = choice.reshape(N_GROUPS, per, tm)
    io3 = lax.broadcasted_iota(jnp.int32, c3.shape, 1).astype(F32)
    m1 = jnp.max(c3, axis=1, keepdims=True)
    i1 = jnp.min(jnp.where(c3 == m1, io3, big), axis=1, keepdims=True)
    m2 = jnp.max(jnp.where(io3 == i1, -jnp.inf, c3), axis=1, keepdims=True)
    gs = (m1 + m2).reshape(N_GROUPS, tm)
    gio = lax.broadcasted_iota(jnp.int32, gs.shape, 0).astype(F32)
    keep = jnp.zeros(gs.shape, F32)
    for _ in range(TOPK_GROUPS):
        mg = jnp.max(gs, axis=0, keepdims=True)
        ig = jnp.min(jnp.where(gs == mg, gio, big), axis=0, keepdims=True)
        sel = gio == ig
        keep = jnp.where(sel, 1.0, keep)
        gs = jnp.where(sel, -jnp.inf, gs)
    cur = jnp.where(keep.reshape(N_GROUPS, 1, tm) > 0.5, c3, NEG_INF).reshape(E, tm)
    eio = lax.broadcasted_iota(jnp.int32, (E, tm), 0).astype(F32)
    sels, idxs, ws = [], [], []
    for _ in range(TOP_K):
        m = jnp.max(cur, axis=0, keepdims=True)
        ik = jnp.min(jnp.where(cur == m, eio, big), axis=0, keepdims=True)
        sel = eio == ik
        ws.append(jnp.sum(jnp.where(sel, s, 0.0), axis=0, keepdims=True))
        cur = jnp.where(sel, -jnp.inf, cur)
        sels.append(sel)
        idxs.append(ik)
    wsum = ws[0]
    for wk in ws[1:]:
        wsum = wsum + wk
    w_ref[...] = jnp.concatenate([wk / wsum * ROUTED_SCALE for wk in ws], axis=0)
    idx_ref[...] = jnp.concatenate(idxs, axis=0).astype(jnp.int32)
    onehot = jnp.zeros((E, tm), F32)
    for sel in sels:
        onehot = jnp.where(sel, 1.0, onehot)
    prefix = jnp.dot(onehot.astype(BF16), tri_ref[...], preferred_element_type=F32) + carry_ref[...]
    rank_ref[...] = jnp.concatenate(
        [jnp.sum(jnp.where(sel, prefix, 0.0), axis=0, keepdims=True) for sel in sels], axis=0).astype(jnp.int32)
    carry = carry_ref[...] + jnp.sum(onehot, axis=1, keepdims=True)
    carry_ref[...] = carry
    cnt_ref[...] = jnp.broadcast_to(carry, cnt_ref.shape)


def _route(scores_t, router_b):
    E, N = scores_t.shape
    tm = _tile(N, 512)
    tri = (jnp.arange(tm)[:, None] < jnp.arange(tm)[None, :]).astype(BF16)
    return pl.pallas_call(
        _route_kernel,
        out_shape=(jax.ShapeDtypeStruct((TOP_K, N), jnp.int32), jax.ShapeDtypeStruct((TOP_K, N), F32),
                   jax.ShapeDtypeStruct((TOP_K, N), jnp.int32), jax.ShapeDtypeStruct((E, LANES), F32)),
        grid=(N // tm,),
        in_specs=[pl.BlockSpec((E, tm), lambda i: (0, i)), pl.BlockSpec((E, 1), lambda i: (0, 0)),
                  pl.BlockSpec((tm, tm), lambda i: (0, 0))],
        out_specs=(pl.BlockSpec((TOP_K, tm), lambda i: (0, i)), pl.BlockSpec((TOP_K, tm), lambda i: (0, i)),
                   pl.BlockSpec((TOP_K, tm), lambda i: (0, i)), pl.BlockSpec((E, LANES), lambda i: (0, 0))),
        scratch_shapes=[pltpu.VMEM((E, 1), F32)],
        compiler_params=_cparams(("arbitrary",)),
        name="route_topk",
    )(scores_t, router_b.reshape(E, 1), tri)


def _dispatch_kernel(dest_ref, u_ref, xs_ref, sem):
    tm = u_ref.shape[0]

    def body(t, carry):
        for k in range(TOP_K):
            pltpu.make_async_copy(u_ref.at[pl.ds(t, 1)], xs_ref.at[pl.ds(dest_ref[k, t], 1)], sem).start()
        return carry

    lax.fori_loop(0, tm, body, 0)
    for k in range(TOP_K):
        pltpu.make_async_copy(u_ref, xs_ref.at[pl.ds(0, tm)], sem).wait()


def _dispatch(dest, u2, n_slots):
    N, D = u2.shape
    tm = _tile(N, 256)
    return pl.pallas_call(
        _dispatch_kernel,
        out_shape=jax.ShapeDtypeStruct((n_slots, D), F32),
        grid=(N // tm,),
        in_specs=[pl.BlockSpec((TOP_K, tm), lambda i: (0, i), memory_space=pltpu.SMEM),
                  pl.BlockSpec((tm, D), lambda i: (i, 0))],
        out_specs=pl.BlockSpec(memory_space=pl.ANY),
        scratch_shapes=[pltpu.SemaphoreType.DMA(())],
        compiler_params=pltpu.CompilerParams(dimension_semantics=("arbitrary",), vmem_limit_bytes=VMEM_LIMIT,
                                             has_side_effects=True),
        name="moe_dispatch",
    )(dest, u2)


def _expert_kernel(wb_ref, we_ref, lo_ref, hi_ref, nw_ref, xs_ref, w1_ref, w3_ref, w2_ref, ys_ref, w1b, w3b, w2b):
    j = pl.program_id(0)

    @pl.when(j < nw_ref[0])
    def _():
        prev = jnp.maximum(j - 1, 0)
        new_expert = jnp.logical_or(j == 0, we_ref[j] != we_ref[prev])
        new_block = jnp.logical_or(j == 0, wb_ref[j] != wb_ref[prev])

        @pl.when(new_expert)
        def _():
            w1b[...] = w1_ref[0].astype(BF16)
            w3b[...] = w3_ref[0].astype(BF16)
            w2b[...] = w2_ref[0].astype(BF16)

        x = xs_ref[...]
        rows = lax.broadcasted_iota(jnp.int32, x.shape, 0)
        xb = jnp.where((rows >= lo_ref[j]) & (rows < hi_ref[j]), x, 0.0).astype(BF16)
        h1 = jnp.dot(xb, w1b[...], preferred_element_type=F32)
        h3 = jnp.dot(xb, w3b[...], preferred_element_type=F32)
        hid = (h1 * _sigmoid(h1) * h3).astype(BF16)
        y = jnp.dot(hid, w2b[...], preferred_element_type=F32)

        @pl.when(new_block)
        def _():
            ys_ref[...] = y

        @pl.when(jnp.logical_not(new_block))
        def _():
            ys_ref[...] += y


def _experts(wb, we, lo, hi, nw, xs, w1, w3, w2):
    n_slots, D = xs.shape
    E, _, F = w1.shape
    blk = lambda j, wb, we, lo, hi, nw: (wb[j], 0)
    wsel = lambda j, wb, we, lo, hi, nw: (we[j], 0, 0)
    grid_spec = pltpu.PrefetchScalarGridSpec(
        num_scalar_prefetch=5, grid=(wb.shape[0],),
        in_specs=[pl.BlockSpec((MOE_BM, D), blk), pl.BlockSpec((1, D, F), wsel), pl.BlockSpec((1, D, F), wsel),
                  pl.BlockSpec((1, F, D), wsel)],
        out_specs=pl.BlockSpec((MOE_BM, D), blk),
        scratch_shapes=[pltpu.VMEM((D, F), BF16), pltpu.VMEM((D, F), BF16), pltpu.VMEM((F, D), BF16)])
    return pl.pallas_call(
        _expert_kernel, out_shape=jax.ShapeDtypeStruct((n_slots, D), F32), grid_spec=grid_spec,
        compiler_params=_cparams(("arbitrary",)), name="moe_experts",
    )(wb, we, lo, hi, nw, xs, w1, w3, w2)


def _combine_kernel(dest_ref, w_ref, xs1_ref, g2_ref, fg_ref, ys_ref, o_ref, buf, sem, *, final):
    tm = xs1_ref.shape[0]

    def body(t, carry):
        for k in range(TOP_K):
            pltpu.make_async_copy(ys_ref.at[pl.ds(dest_ref[k, t], 1)], buf.at[k, pl.ds(t, 1)], sem).start()
        return carry

    lax.fori_loop(0, tm, body, 0)
    for k in range(TOP_K):
        pltpu.make_async_copy(ys_ref.at[pl.ds(0, tm)], buf.at[k], sem).wait()
    w = w_ref[...]
    r = w[:, 0:1] * buf[0]
    for k in range(1, TOP_K):
        r = r + w[:, k:k + 1] * buf[k]
    x2 = xs1_ref[...] + g2_ref[0] * r
    if final:
        x2 = x2 * lax.rsqrt(jnp.mean(x2 * x2, axis=-1, keepdims=True) + EPS) * fg_ref[...]
    o_ref[...] = x2


def _combine(dest, w_tok, xs1, g2, fg, ys, *, seq, final):
    N, D = xs1.shape
    tm = _tile(seq, 128)
    per_b = seq // tm
    return pl.pallas_call(
        functools.partial(_combine_kernel, final=final),
        out_shape=jax.ShapeDtypeStruct((N, D), F32),
        grid=(N // tm,),
        in_specs=[pl.BlockSpec((TOP_K, tm), lambda i: (0, i), memory_space=pltpu.SMEM),
                  pl.BlockSpec((tm, TOP_K), lambda i: (i, 0)),
                  pl.BlockSpec((tm, D), lambda i: (i, 0)),
                  pl.BlockSpec((1, 1, D), lambda i: (i // per_b, 0, 0)),
                  pl.BlockSpec((1, D), lambda i: (0, 0)),
                  pl.BlockSpec(memory_space=pl.ANY)],
        out_specs=pl.BlockSpec((tm, D), lambda i: (i, 0)),
        scratch_shapes=[pltpu.VMEM((TOP_K, tm, D), F32), pltpu.SemaphoreType.DMA(())],
        compiler_params=_cparams(("arbitrary",)),
        name="moe_combine",
    )(dest, w_tok, xs1, g2, fg, ys)


def _rope_tables(S):
    half = HEAD_DIM // 2
    inv = ROPE_THETA ** (-jnp.arange(0, half, 2, dtype=F32) / half)
    t = jnp.arange(S, dtype=jnp.int32)

    def part(pos):
        ang = pos.astype(F32)[:, None] * inv[None, :]
        c, s, z = jnp.cos(ang), jnp.sin(ang), jnp.zeros_like(ang)
        return jnp.concatenate([c, c], -1), jnp.concatenate([-s, z], -1), jnp.concatenate([z, s], -1)

    row, col, lin = part(t // GRID_W), part(t % GRID_W), part(t)
    tabs_a = tuple(jnp.tile(jnp.concatenate([r, c], -1), (1, LANES // HEAD_DIM)) for r, c in zip(row, col))
    tabs_c = tuple(jnp.pad(v, ((0, 0), (0, LANES - C_ROPE))) for v in lin)
    return tabs_a, tabs_c


def _head_slabs(w, heads, groups):
    d = w.shape[0]
    hot = (jnp.arange(heads)[:, None] // (heads // groups) == jnp.arange(groups)[None, :]).astype(w.dtype)
    return (w.reshape(d, heads, 1, HEAD_DIM) * hot[None, :, :, None]).reshape(d, heads * LANES)


def _pad_w_in(w):
    d = w.shape[0]
    a, b, c = w[:, :A_IN], w[:, A_IN:A_IN + B_IN], w[:, A_IN + B_IN:]
    kr = jnp.pad(c[:, C_Q_LORA + C_KV_LORA:], ((0, 0), (0, LANES - C_ROPE)))
    return jnp.concatenate([_head_slabs(a[:, :A_WIDTH], A_HEADS, A_KV_HEADS), a[:, A_WIDTH:],
                            _head_slabs(b[:, :B_WIDTH], B_HEADS, B_KV_HEADS), b[:, B_WIDTH:],
                            c[:, :C_Q_LORA + C_KV_LORA], kr], axis=1).astype(BF16)


def _pad_w_uq(w):
    r = w.shape[0]
    w = w.reshape(r, C_HEADS, C_NOPE + C_ROPE)
    hot = (jnp.arange(C_HEADS)[:, None] % 2 == jnp.arange(2)[None, :]).astype(w.dtype)
    nope = (w[:, :, None, :C_NOPE] * hot[None, :, :, None]).reshape(r, C_HEADS, LANES)
    rope = jnp.pad(w[:, :, C_NOPE:], ((0, 0), (0, 0), (0, LANES - C_ROPE)))
    return jnp.concatenate([nope, rope], axis=-1).reshape(r, C_HEADS * 2 * LANES).astype(BF16)


def _select_mats(heads, groups, out_w):
    h = jnp.arange(heads)[:, None, None]
    r = jnp.arange(LANES)[None, :, None]
    c = jnp.arange(out_w)[None, None, :]
    half = h // (heads // groups)
    return ((r // HEAD_DIM == half) & (c == HEAD_DIM * h + r % HEAD_DIM)).astype(BF16)


def kernel(x, c, mod_w, mod_b, norm1_g, norm2_g, w_in, a_qnorm_g, a_knorm_g, b_sink, c_qnorm_g, c_kvnorm_g, c_w_uq, c_w_uk, c_w_uv, out_norm_g, w_out, router_w, router_b, exp_w1, exp_w3, exp_w2, sh_w1, sh_w3, sh_w2, final_g):
    B, S, D = x.shape
    L = mod_w.shape[0]
    E = router_w.shape[2]
    N = B * S
    n_slots = N * TOP_K
    assert n_slots % MOE_BM == 0
    n_work = n_slots // MOE_BM + E

    mod = _mod(c, mod_w, mod_b).reshape(L, B, 6, 1, D)
    tabs_a, tabs_c = _rope_tables(S)
    esel_a = _select_mats(A_HEADS, A_KV_HEADS, A_WIDTH)
    esel_c = _select_mats(2, 2, LANES)
    slopes = (jnp.exp2(-8.0 * jnp.arange(1, B_HEADS + 1, dtype=F32) / B_HEADS) * LOG2E).reshape(B_HEADS, 1, 1)

    for l in range(L):
        sh1, sc1, g1, sh2, sc2, g2 = [mod[l, :, i] for i in range(6)]
        gq2 = jnp.tile(a_qnorm_g[l], LANES // HEAD_DIM).reshape(1, LANES)
        gk2 = jnp.tile(a_knorm_g[l], LANES // HEAD_DIM).reshape(1, LANES)
        qa, ka, va, qb, kb, vb, qc, kc, vc = _inproj(
            x, sc1, sh1, norm1_g[l].reshape(1, D), _pad_w_in(w_in[l]), tabs_a, tabs_c, gq2, gk2,
            c_qnorm_g[l].reshape(1, -1), c_kvnorm_g[l].reshape(1, -1), _pad_w_uq(c_w_uq[l]),
            c_w_uk[l].astype(BF16), c_w_uv[l].astype(BF16))
        oa = _flash(qa[:, None], ka[:, None], va[:, None], esel_a, tq=256, tk=512)
        ob = _window(qb, kb, vb, esel_a, slopes, (b_sink[l].astype(F32) * LOG2E).reshape(B_HEADS, 1, 1), tq=256)
        oc = _flash(qc.reshape(B, C_HEADS // 2, 2, S, 2 * LANES), kc, vc, esel_c, tq=512, tk=512)
        xs1, u2, scores_t = _post(
            oa, ob, oc, x, g1, sc2, sh2, g2, out_norm_g[l].reshape(1, -1), w_out[l].astype(BF16),
            norm2_g[l].reshape(1, D), router_w[l].T.astype(BF16), sh_w1[l].astype(BF16), sh_w3[l].astype(BF16),
            sh_w2[l].astype(BF16))
        idx, w_top, rank, cnt = _route(scores_t, router_b[l])
        counts = cnt[:, 0].astype(jnp.int32)
        ends = jnp.cumsum(counts)
        starts = ends - counts
        dest = starts[idx] + rank
        first_blk = starts // MOE_BM
        n_items = jnp.where(counts > 0, (ends - 1) // MOE_BM - first_blk + 1, 0)
        item_ends = jnp.cumsum(n_items)
        nw = item_ends[-1].reshape(1).astype(jnp.int32)
        wid = jnp.minimum(jnp.arange(n_work, dtype=jnp.int32), nw[0] - 1)
        we = jnp.minimum(jnp.searchsorted(item_ends, wid, side='right'), E - 1).astype(jnp.int32)
        wb = (first_blk[we] + wid - (item_ends - n_items)[we]).astype(jnp.int32)
        lo = jnp.clip(starts[we] - wb * MOE_BM, 0, MOE_BM).astype(jnp.int32)
        hi = jnp.clip(ends[we] - wb * MOE_BM, 0, MOE_BM).astype(jnp.int32)
        xs = _dispatch(dest, u2, n_slots)
        ys = _experts(wb, we, lo, hi, nw, xs, exp_w1[l], exp_w3[l], exp_w2[l])
        x = _combine(dest, w_top.T, xs1.reshape(N, D), g2, final_g.reshape(1, D), ys, seq=S,
                     final=(l == L - 1)).reshape(B, S, D)
    return x
```

```python
import functools

import jax
import jax.numpy as jnp
from jax import lax
from jax.experimental import pallas as pl
from jax.experimental.pallas import tpu as pltpu

F32 = jnp.float32
BF16 = jnp.bfloat16

HEAD_DIM = 64
GRID_W = 64
ROPE_THETA = 10000.0
EPS = 1e-6
NEG_INF = -1e30
A_HEADS, A_KV_HEADS = 6, 2
B_HEADS, B_KV_HEADS = 6, 2
WINDOW = 128
C_HEADS, C_Q_LORA, C_KV_LORA, C_NOPE, C_ROPE, C_V = 4, 256, 128, 64, 32, 64
A_WIDTH = A_HEADS * HEAD_DIM
B_WIDTH = B_HEADS * HEAD_DIM
C_WIDTH = C_HEADS * C_V
A_IN = (A_HEADS + 2 * A_KV_HEADS) * HEAD_DIM
B_IN = (B_HEADS + 2 * B_KV_HEADS) * HEAD_DIM
TOP_K = 8
N_GROUPS = 8
TOPK_GROUPS = 4
ROUTED_SCALE = 2.5
LOG2E = 1.4426950408889634

LANES = 128
SUBLANE_BITS = 3
SUBLANES = 1 << SUBLANE_BITS
OFF_AQ = 0
OFF_AK = OFF_AQ + A_HEADS * LANES
OFF_AV = OFF_AK + LANES
OFF_BQ = OFF_AV + LANES
OFF_BK = OFF_BQ + B_HEADS * LANES
OFF_BV = OFF_BK + LANES
OFF_CQ = OFF_BV + LANES
OFF_CKV = OFF_CQ + C_Q_LORA
OFF_CKR = OFF_CKV + C_KV_LORA
IN_PAD = OFF_CKR + LANES

ONES_ROWS = 16
VT_ROWS = LANES + ONES_ROWS
MOE_BM = 256
VMEM_LIMIT = 56 * 1024 * 1024


def _tile(n, pref):
    t = min(n, pref)
    assert n % t == 0, (n, t)
    return t


def _cparams(sem, vmem=VMEM_LIMIT):
    return pltpu.CompilerParams(dimension_semantics=sem, vmem_limit_bytes=vmem)


def _mod_kernel(c_ref, w_ref, b_ref, o_ref):
    c = c_ref[...]
    ca = c * (1.0 / (1.0 + jnp.exp(-c)))
    o_ref[0] = jnp.dot(ca, w_ref[0], preferred_element_type=F32, precision=lax.Precision.HIGHEST) + b_ref[0]


def _mod(c, mod_w, mod_b):
    L, D, W = mod_w.shape
    B = c.shape[0]
    tn = _tile(W, 1536)
    return pl.pallas_call(
        _mod_kernel,
        out_shape=jax.ShapeDtypeStruct((L, B, W), F32),
        grid=(L, W // tn),
        in_specs=[
            pl.BlockSpec((B, D), lambda l, j: (0, 0)),
            pl.BlockSpec((1, D, tn), lambda l, j: (l, 0, j)),
            pl.BlockSpec((1, 1, tn), lambda l, j: (l, 0, j)),
        ],
        out_specs=pl.BlockSpec((1, B, tn), lambda l, j: (l, 0, j)),
        compiler_params=_cparams(("arbitrary", "arbitrary")),
        name="mod_proj",
    )(c, mod_w, mod_b.reshape(L, 1, W))


def _rope(v, c, sa, sb):
    return v * c + pltpu.roll(v, LANES - 16, 1) * sa + pltpu.roll(v, 16, 1) * sb


def _inproj_kernel(x_ref, sc_ref, sh_ref, g_ref, w_ref, ca_ref, saa_ref, sba_ref, cc_ref, sac_ref, sbc_ref,
                   gq_ref, gk_ref, gcq_ref, gckv_ref, wuq_ref, wuk_ref, wuv_ref,
                   qa_ref, ka_ref, va_ref, qb_ref, kb_ref, vb_ref, qc_ref, kc_ref, vc_ref, *, qs_ab, qs_c):
    x = x_ref[0]
    u = x * lax.rsqrt(jnp.mean(x * x, axis=-1, keepdims=True) + EPS) * g_ref[...]
    u = u * (1.0 + sc_ref[0]) + sh_ref[0]
    p = jnp.dot(u.astype(BF16), w_ref[...], preferred_element_type=F32)
    ca, saa, sba = ca_ref[...], saa_ref[...], sba_ref[...]
    cc, sac, sbc = cc_ref[...], sac_ref[...], sbc_ref[...]

    gq = gq_ref[...]
    for h in range(A_HEADS):
        v = p[:, OFF_AQ + LANES * h:OFF_AQ + LANES * (h + 1)]
        ss = jnp.sum(v * v, axis=-1, keepdims=True) * (1.0 / HEAD_DIM)
        v = v * lax.rsqrt(ss + EPS) * gq
        qa_ref[0, h] = (_rope(v, ca, saa, sba) * qs_ab).astype(BF16)
    k = p[:, OFF_AK:OFF_AK + LANES]
    lane = lax.broadcasted_iota(jnp.int32, k.shape, 1)
    k2 = k * k
    s0 = jnp.sum(jnp.where(lane < HEAD_DIM, k2, 0.0), axis=-1, keepdims=True)
    s1 = jnp.sum(k2, axis=-1, keepdims=True) - s0
    r = jnp.where(lane < HEAD_DIM, lax.rsqrt(s0 * (1.0 / HEAD_DIM) + EPS), lax.rsqrt(s1 * (1.0 / HEAD_DIM) + EPS))
    ka_ref[0] = _rope(k * r * gk_ref[...], ca, saa, sba).astype(BF16)
    ones = jnp.ones((ONES_ROWS, k.shape[0]), BF16)
    va_ref[0] = jnp.concatenate([p[:, OFF_AV:OFF_AV + LANES].T.astype(BF16), ones], axis=0)

    for h in range(B_HEADS):
        qb_ref[0, h] = (p[:, OFF_BQ + LANES * h:OFF_BQ + LANES * (h + 1)] * qs_ab).astype(BF16)
    kb_ref[0] = p[:, OFF_BK:OFF_BK + LANES].astype(BF16)
    vb_ref[0] = p[:, OFF_BV:OFF_BV + LANES].astype(BF16)

    cq = p[:, OFF_CQ:OFF_CQ + C_Q_LORA]
    cq = cq * lax.rsqrt(jnp.mean(cq * cq, axis=-1, keepdims=True) + EPS) * gcq_ref[...]
    qh = jnp.dot(cq.astype(BF16), wuq_ref[...], preferred_element_type=F32)
    for h in range(C_HEADS):
        base = 2 * LANES * h
        qc_ref[0, h, :, 0:LANES] = (qh[:, base:base + LANES] * qs_c).astype(BF16)
        qc_ref[0, h, :, LANES:2 * LANES] = (_rope(qh[:, base + LANES:base + 2 * LANES], cc, sac, sbc) * qs_c).astype(BF16)
    ckv = p[:, OFF_CKV:OFF_CKV + C_KV_LORA]
    ckv = (ckv * lax.rsqrt(jnp.mean(ckv * ckv, axis=-1, keepdims=True) + EPS) * gckv_ref[...]).astype(BF16)
    kn = jnp.dot(ckv, wuk_ref[...], preferred_element_type=F32)
    vv = jnp.dot(ckv, wuv_ref[...], preferred_element_type=F32)
    kr = _rope(p[:, OFF_CKR:OFF_CKR + LANES], cc, sac, sbc).astype(BF16)
    for pr in range(C_HEADS // 2):
        kc_ref[0, pr, :, 0:LANES] = kn[:, LANES * pr:LANES * (pr + 1)].astype(BF16)
        kc_ref[0, pr, :, LANES:2 * LANES] = kr
        vc_ref[0, pr] = jnp.concatenate([vv[:, LANES * pr:LANES * (pr + 1)].T.astype(BF16), ones], axis=0)


def _inproj(x, sc1, sh1, g, w_pad, tabs_a, tabs_c, gq, gk, gcq, gckv, wuq_pad, wuk, wuv):
    B, S, D = x.shape
    tm = _tile(S, 512)
    full = lambda shp: pl.BlockSpec(shp, lambda b, i: (0,) * len(shp))
    tab = pl.BlockSpec((tm, LANES), lambda b, i: (i, 0))
    mod = pl.BlockSpec((1, 1, D), lambda b, i: (b, 0, 0))
    kern = functools.partial(_inproj_kernel, qs_ab=HEAD_DIM ** -0.5 * LOG2E, qs_c=(C_NOPE + C_ROPE) ** -0.5 * LOG2E)
    outs = (
        jax.ShapeDtypeStruct((B, A_HEADS, S, LANES), BF16), jax.ShapeDtypeStruct((B, S, LANES), BF16),
        jax.ShapeDtypeStruct((B, VT_ROWS, S), BF16),
        jax.ShapeDtypeStruct((B, B_HEADS, S, LANES), BF16), jax.ShapeDtypeStruct((B, S, LANES), BF16),
        jax.ShapeDtypeStruct((B, S, LANES), BF16),
        jax.ShapeDtypeStruct((B, C_HEADS, S, 2 * LANES), BF16), jax.ShapeDtypeStruct((B, C_HEADS // 2, S, 2 * LANES), BF16),
        jax.ShapeDtypeStruct((B, C_HEADS // 2, VT_ROWS, S), BF16),
    )
    slab = pl.BlockSpec((1, tm, LANES), lambda b, i: (b, i, 0))
    out_specs = (
        pl.BlockSpec((1, A_HEADS, tm, LANES), lambda b, i: (b, 0, i, 0)), slab,
        pl.BlockSpec((1, VT_ROWS, tm), lambda b, i: (b, 0, i)),
        pl.BlockSpec((1, B_HEADS, tm, LANES), lambda b, i: (b, 0, i, 0)), slab, slab,
        pl.BlockSpec((1, C_HEADS, tm, 2 * LANES), lambda b, i: (b, 0, i, 0)),
        pl.BlockSpec((1, C_HEADS // 2, tm, 2 * LANES), lambda b, i: (b, 0, i, 0)),
        pl.BlockSpec((1, C_HEADS // 2, VT_ROWS, tm), lambda b, i: (b, 0, 0, i)),
    )
    return pl.pallas_call(
        kern, out_shape=outs, grid=(B, S // tm),
        in_specs=[pl.BlockSpec((1, tm, D), lambda b, i: (b, i, 0)), mod, mod, full((1, D)), full((D, IN_PAD)),
                  tab, tab, tab, tab, tab, tab,
                  full((1, LANES)), full((1, LANES)), full((1, C_Q_LORA)), full((1, C_KV_LORA)),
                  full(wuq_pad.shape), full(wuk.shape), full(wuv.shape)],
        out_specs=out_specs,
        compiler_params=_cparams(("arbitrary", "arbitrary")),
        name="norm_inproj",
    )(x, sc1, sh1, g, w_pad, *tabs_a, *tabs_c, gq, gk, gcq, gckv, wuq_pad, wuk, wuv)


def _flash_kernel(q_ref, k_ref, vt_ref, e_ref, o_ref, qt_ref, m_ref, acc_ref, *, tk, cw, unroll):
    H, tq, dk = q_ref.shape[2:]
    S = k_ref.shape[2]
    R = H * tq
    for h in range(H):
        qt_ref[:, h * tq:(h + 1) * tq] = q_ref[0, 0, h].astype(F32).T.astype(BF16)
    m_ref[...] = jnp.full(m_ref.shape, NEG_INF, F32)
    acc_ref[...] = jnp.zeros(acc_ref.shape, F32)

    def body(j, carry):
        off = pl.multiple_of(j * tk, tk)
        k = k_ref[0, 0, pl.ds(off, tk), :]
        vt = vt_ref[0, 0, :, pl.ds(off, tk)]
        nc = R // cw
        scores = lambda c: jnp.dot(k, qt_ref[:, c * cw:(c + 1) * cw], preferred_element_type=F32)
        accs, ms = [], []
        s_next = scores(0)
        for c in range(nc):
            sl = slice(c * cw, (c + 1) * cw)
            s = s_next
            if c + 1 < nc:
                s_next = scores(c + 1)
            m_prev = m_ref[:, sl]
            m_new = jnp.maximum(m_prev, jnp.max(s, axis=0, keepdims=True))
            p = jnp.exp2(s - m_new).astype(BF16)
            alpha = jnp.exp2(m_prev - m_new)
            accs.append(alpha * acc_ref[:, sl] + jnp.dot(vt, p, preferred_element_type=F32))
            ms.append(m_new)
        acc_ref[...] = jnp.concatenate(accs, axis=1)
        m_ref[...] = jnp.concatenate(ms, axis=1)
        return carry

    lax.fori_loop(0, S // tk, body, 0, unroll=unroll)
    o_t = acc_ref[0:LANES, :] / acc_ref[LANES:LANES + 1, :]
    out = None
    for h in range(H):
        o_h = o_t[:, h * tq:(h + 1) * tq].T.astype(BF16)
        term = jnp.dot(o_h, e_ref[h], preferred_element_type=F32)
        out = term if out is None else out + term
    o_ref[0] = out.astype(BF16)


def _flash(q, k, vt, esel, *, tq, tk, cw=256, unroll=2):
    B, P, H, S, dk = q.shape
    W = esel.shape[2]
    tq = _tile(S, tq)
    tk = _tile(S, tk)
    R = H * tq
    cw = _tile(R, cw)
    return pl.pallas_call(
        functools.partial(_flash_kernel, tk=tk, cw=cw, unroll=unroll),
        out_shape=jax.ShapeDtypeStruct((B, S, P * W), BF16),
        grid=(B, P, S // tq),
        in_specs=[pl.BlockSpec((1, 1, H, tq, dk), lambda b, p, i: (b, p, 0, i, 0)),
                  pl.BlockSpec((1, 1, S, dk), lambda b, p, i: (b, p, 0, 0)),
                  pl.BlockSpec((1, 1, VT_ROWS, S), lambda b, p, i: (b, p, 0, 0)),
                  pl.BlockSpec((H, LANES, W), lambda b, p, i: (0, 0, 0))],
        out_specs=pl.BlockSpec((1, tq, W), lambda b, p, i: (b, i, p)),
        scratch_shapes=[pltpu.VMEM((dk, R), BF16), pltpu.VMEM((1, R), F32), pltpu.VMEM((VT_ROWS, R), F32)],
        compiler_params=_cparams(("arbitrary", "arbitrary", "arbitrary")),
        name="flash_attn",
    )(q, k, vt, esel)


def _window_kernel(q_ref, k_ref, v_ref, e_ref, slope_ref, sink_ref, o_ref, *, tq):
    H = q_ref.shape[1]
    S = k_ref.shape[1]
    wk = min(S, tq + 2 * WINDOW)
    i = pl.program_id(1)
    start = pl.multiple_of(jnp.clip(i * tq - WINDOW, 0, S - wk), WINDOW)
    q = q_ref[0].reshape(H * tq, LANES)
    k = k_ref[0, pl.ds(start, wk), :]
    v = v_ref[0, pl.ds(start, wk), :]
    s = lax.dot_general(q, k, (((1,), (1,)), ((), ())), preferred_element_type=F32).reshape(H, tq, wk)
    tpos = i * tq + lax.broadcasted_iota(jnp.int32, (tq, wk), 0)
    spos = start + lax.broadcasted_iota(jnp.int32, (tq, wk), 1)
    dist = jnp.abs(spos - tpos)
    s = s - slope_ref[...] * dist.astype(F32)[None]
    s = jnp.where((dist <= WINDOW)[None], s, NEG_INF)
    sink = sink_ref[...]
    m = jnp.maximum(jnp.max(s, axis=-1, keepdims=True), sink)
    e = jnp.exp2(s - m)
    den = jnp.sum(e, axis=-1, keepdims=True) + jnp.exp2(sink - m)
    o = jnp.dot(e.reshape(H * tq, wk).astype(BF16), v, preferred_element_type=F32).reshape(H, tq, LANES)
    o = (o / den).astype(BF16)
    out = jnp.dot(o[0], e_ref[0], preferred_element_type=F32)
    for h in range(1, H):
        out = out + jnp.dot(o[h], e_ref[h], preferred_element_type=F32)
    o_ref[0] = out.astype(BF16)


def _window(q, k, v, esel, slopes, sink, *, tq):
    B, H, S, _ = q.shape
    W = esel.shape[2]
    tq = _tile(S, tq)
    return pl.pallas_call(
        functools.partial(_window_kernel, tq=tq),
        out_shape=jax.ShapeDtypeStruct((B, S, W), BF16),
        grid=(B, S // tq),
        in_specs=[pl.BlockSpec((1, H, tq, LANES), lambda b, i: (b, 0, i, 0)),
                  pl.BlockSpec((1, S, LANES), lambda b, i: (b, 0, 0)),
                  pl.BlockSpec((1, S, LANES), lambda b, i: (b, 0, 0)),
                  pl.BlockSpec((H, LANES, W), lambda b, i: (0, 0, 0)),
                  pl.BlockSpec((H, 1, 1), lambda b, i: (0, 0, 0)),
                  pl.BlockSpec((H, 1, 1), lambda b, i: (0, 0, 0))],
        out_specs=pl.BlockSpec((1, tq, W), lambda b, i: (b, i, 0)),
        compiler_params=_cparams(("arbitrary", "arbitrary")),
        name="window_attn",
    )(q, k, v, esel, slopes, sink)


def _sigmoid(v):
    return 1.0 / (1.0 + jnp.exp(-v))


def _pack_pairs(x):
    w = x.shape[1] // 2
    lo = lax.bitcast_convert_type(x[:, :w].astype(BF16).astype(F32), jnp.uint32)
    hi = lax.bitcast_convert_type(x[:, w:].astype(BF16).astype(F32), jnp.uint32)
    return (lo >> 16) | hi


def _unpack_pairs(u):
    lo = lax.bitcast_convert_type(u << 16, F32)
    hi = lax.bitcast_convert_type(u & jnp.uint32(0xFFFF0000), F32)
    return jnp.concatenate([lo, hi], axis=1)


def _post_kernel(oa_ref, ob_ref, oc_ref, x_ref, g1_ref, sc2_ref, sh2_ref, g2_ref, og_ref, wout_ref, n2_ref, rwt_ref,
                 sw1_ref, sw3_ref, sw2_ref, xs1_ref, u2_ref, st_ref):
    og = og_ref[...]

    def rn(o_ref, g):
        o = o_ref[0].astype(F32)
        return (o * lax.rsqrt(jnp.mean(o * o, axis=-1, keepdims=True) + EPS) * g).astype(BF16)

    mixed = jnp.concatenate([rn(oa_ref, og[:, :A_WIDTH]), rn(ob_ref, og[:, A_WIDTH:A_WIDTH + B_WIDTH]),
                             rn(oc_ref, og[:, A_WIDTH + B_WIDTH:])], axis=-1)
    x1 = x_ref[0] + g1_ref[0] * jnp.dot(mixed, wout_ref[...], preferred_element_type=F32)
    u2 = x1 * lax.rsqrt(jnp.mean(x1 * x1, axis=-1, keepdims=True) + EPS) * n2_ref[...]
    u2 = u2 * (1.0 + sc2_ref[0]) + sh2_ref[0]
    u2_ref[...] = _pack_pairs(u2)
    ub = u2.astype(BF16)
    logits_t = lax.dot_general(rwt_ref[...], ub, (((1,), (1,)), ((), ())), preferred_element_type=F32)
    st_ref[...] = _sigmoid(logits_t)
    h1 = jnp.dot(ub, sw1_ref[...], preferred_element_type=F32)
    h3 = jnp.dot(ub, sw3_ref[...], preferred_element_type=F32)
    hid = (h1 * _sigmoid(h1) * h3).astype(BF16)
    xs1_ref[0] = x1 + g2_ref[0] * jnp.dot(hid, sw2_ref[...], preferred_element_type=F32)


def _post(oa, ob, oc, x, g1, sc2, sh2, g2, og, wout, n2, rwt, sw1, sw3, sw2):
    B, S, D = x.shape
    E = rwt.shape[0]
    tm = _tile(S, 512)
    nt = S // tm
    full = lambda a: pl.BlockSpec(a.shape, lambda b, i: (0,) * a.ndim)
    mod = pl.BlockSpec((1, 1, D), lambda b, i: (b, 0, 0))
    row = lambda w: pl.BlockSpec((1, tm, w), lambda b, i: (b, i, 0))
    return pl.pallas_call(
        _post_kernel,
        out_shape=(jax.ShapeDtypeStruct((B, S, D), F32), jax.ShapeDtypeStruct((B * S, D // 2), jnp.uint32),
                   jax.ShapeDtypeStruct((E, B * S), F32)),
        grid=(B, nt),
        in_specs=[row(A_WIDTH), row(B_WIDTH), row(C_WIDTH), row(D), mod, mod, mod, mod, full(og), full(wout), full(n2),
                  full(rwt), full(sw1), full(sw3), full(sw2)],
        out_specs=(row(D), pl.BlockSpec((tm, D // 2), lambda b, i: (b * nt + i, 0)),
                   pl.BlockSpec((E, tm), lambda b, i: (0, b * nt + i))),
        compiler_params=_cparams(("arbitrary", "arbitrary")),
        name="post_attn",
    )(oa, ob, oc, x, g1, sc2, sh2, g2, og, wout, n2, rwt, sw1, sw3, sw2)


def _route_kernel(st_ref, rb_ref, tri_ref, idx_ref, w_ref, rank_ref, cnt_ref, carry_ref):
    E, tm = st_ref.shape
    per = E // N_GROUPS
    big = float(E)

    @pl.when(pl.program_id(0) == 0)
    def _():
        carry_ref[...] = jnp.zeros(carry_ref.shape, F32)

    s = st_ref[...]
    choice = s + rb_ref[...]
    c3 = choice.reshape(N_GROUPS, per, tm)
    io3 = lax.broadcasted_iota(jnp.int32, c3.shape, 1).astype(F32)
    m1 = jnp.max(c3, axis=1, keepdims=True)
    i1 = jnp.min(jnp.where(c3 == m1, io3, big), axis=1, keepdims=True)
    m2 = jnp.max(jnp.where(io3 == i1, -jnp.inf, c3), axis=1, keepdims=True)
    gs = (m1 + m2).reshape(N_GROUPS, tm)
    gio = lax.broadcasted_iota(jnp.int32, gs.shape, 0).astype(F32)
    keep = jnp.zeros(gs.shape, F32)
    for _ in range(TOPK_GROUPS):
        mg = jnp.max(gs, axis=0, keepdims=True)
        ig = jnp.min(jnp.where(gs == mg, gio, big), axis=0, keepdims=True)
        sel = gio == ig
        keep = jnp.where(sel, 1.0, keep)
        gs = jnp.where(sel, -jnp.inf, gs)
    cur = jnp.where(keep.reshape(N_GROUPS, 1, tm) > 0.5, c3, NEG_INF).reshape(E, tm)
    eio = lax.broadcasted_iota(jnp.int32, (E, tm), 0).astype(F32)
    sels, idxs, ws = [], [], []
    for _ in range(TOP_K):
        m = jnp.max(cur, axis=0, keepdims=True)
        ik = jnp.min(jnp.where(cur == m, eio, big), axis=0, keepdims=True)
        sel = eio == ik
        ws.append(jnp.sum(jnp.where(sel, s, 0.0), axis=0, keepdims=True))
        cur = jnp.where(sel, -jnp.inf, cur)
        sels.append(sel)
        idxs.append(ik)
    wsum = ws[0]
    for wk in ws[1:]:
        wsum = wsum + wk
    w_ref[...] = jnp.concatenate([wk / wsum * ROUTED_SCALE for wk in ws], axis=0)
    idx_ref[...] = jnp.concatenate(idxs, axis=0).astype(jnp.int32)
    onehot = jnp.zeros((E, tm), F32)
    for sel in sels:
        onehot = jnp.where(sel, 1.0, onehot)
    prefix = jnp.dot(onehot.astype(BF16), tri_ref[...], preferred_element_type=F32) + carry_ref[...]
    rank_ref[...] = jnp.concatenate(
        [jnp.sum(jnp.where(sel, prefix, 0.0), axis=0, keepdims=True) for sel in sels], axis=0).astype(jnp.int32)
    carry = carry_ref[...] + jnp.sum(onehot, axis=1, keepdims=True)
    carry_ref[...] = carry
    cnt_ref[...] = jnp.broadcast_to(carry, cnt_ref.shape)


def _route(scores_t, router_b):
    E, N = scores_t.shape
    tm = _tile(N, 512)
    tri = (jnp.arange(tm)[:, None] < jnp.arange(tm)[None, :]).astype(BF16)
    return pl.pallas_call(
        _route_kernel,
        out_shape=(jax.ShapeDtypeStruct((TOP_K, N), jnp.int32), jax.ShapeDtypeStruct((TOP_K, N), F32),
                   jax.ShapeDtypeStruct((TOP_K, N), jnp.int32), jax.ShapeDtypeStruct((E, LANES), F32)),
        grid=(N // tm,),
        in_specs=[pl.BlockSpec((E, tm), lambda i: (0, i)), pl.BlockSpec((E, 1), lambda i: (0, 0)),
                  pl.BlockSpec((tm, tm), lambda i: (0, 0))],
        out_specs=(pl.BlockSpec((TOP_K, tm), lambda i: (0, i)), pl.BlockSpec((TOP_K, tm), lambda i: (0, i)),
                   pl.BlockSpec((TOP_K, tm), lambda i: (0, i)), pl.BlockSpec((E, LANES), lambda i: (0, 0))),
        scratch_shapes=[pltpu.VMEM((E, 1), F32)],
        compiler_params=_cparams(("arbitrary",)),
        name="route_topk",
    )(scores_t, router_b.reshape(E, 1), tri)


def _slots_kernel(idx_ref, rank_ref, starts_ref, dest_ref):
    E = starts_ref.shape[0]
    tm = idx_ref.shape[1]
    eio = lax.broadcasted_iota(jnp.int32, (E, tm), 0)
    starts = starts_ref[...]
    rows = [jnp.sum(jnp.where(eio == idx_ref[k:k + 1, :], starts, 0.0), axis=0, keepdims=True) for k in range(TOP_K)]
    dest_ref[...] = jnp.concatenate(rows, axis=0).astype(jnp.int32) + rank_ref[...]


def _slots(idx, rank, starts):
    K, N = idx.shape
    E = starts.shape[0]
    tm = _tile(N, 512)
    blk = pl.BlockSpec((K, tm), lambda i: (0, i))
    return pl.pallas_call(
        _slots_kernel, out_shape=jax.ShapeDtypeStruct((K, N), jnp.int32), grid=(N // tm,),
        in_specs=[blk, blk, pl.BlockSpec((E, 1), lambda i: (0, 0))], out_specs=blk,
        compiler_params=_cparams(("arbitrary",)), name="moe_slots",
    )(idx, rank, starts.astype(F32).reshape(E, 1))


def _dispatch_kernel(dest_ref, u_ref, xs_ref, sem):
    groups = u_ref.shape[0]

    def body(g, carry):
        for r in range(SUBLANES):
            for k in range(TOP_K):
                d = dest_ref[g * (SUBLANES * TOP_K) + r * TOP_K + k]
                pltpu.make_async_copy(u_ref.at[g, pl.ds(r, 1)],
                                      xs_ref.at[d >> SUBLANE_BITS, pl.ds(d & (SUBLANES - 1), 1)],
                                      sem).start(priority=k % 2)
        return carry

    lax.fori_loop(0, groups, body, 0)
    for k in range(TOP_K):
        pltpu.make_async_copy(u_ref, xs_ref.at[pl.ds(0, groups)], sem).wait()


def _dispatch(dest_flat, u2p, n_slots):
    N, W = u2p.shape
    tm = _tile(N, 256)
    xs = pl.pallas_call(
        _dispatch_kernel,
        out_shape=jax.ShapeDtypeStruct((n_slots // SUBLANES, SUBLANES, W), jnp.uint32),
        grid=(N // tm,),
        in_specs=[pl.BlockSpec((tm * TOP_K,), lambda i: (i,), memory_space=pltpu.SMEM),
                  pl.BlockSpec((tm // SUBLANES, SUBLANES, W), lambda i: (i, 0, 0))],
        out_specs=pl.BlockSpec(memory_space=pl.ANY),
        scratch_shapes=[pltpu.SemaphoreType.DMA(())],
        compiler_params=pltpu.CompilerParams(dimension_semantics=("arbitrary",), vmem_limit_bytes=VMEM_LIMIT,
                                             has_side_effects=True),
        name="moe_dispatch",
    )(dest_flat, u2p.reshape(N // SUBLANES, SUBLANES, W))
    return xs.reshape(n_slots, W)


def _expert_kernel(wb_ref, we_ref, lo_ref, hi_ref, nw_ref, xs_ref, w1_ref, w3_ref, w2_ref, ys_ref, w1b, w3b, w2b):
    j = pl.program_id(0)

    @pl.when(j < nw_ref[0])
    def _():
        prev = jnp.maximum(j - 1, 0)
        new_expert = jnp.logical_or(j == 0, we_ref[j] != we_ref[prev])
        new_block = jnp.logical_or(j == 0, wb_ref[j] != wb_ref[prev])

        @pl.when(new_expert)
        def _():
            w1b[...] = w1_ref[0].astype(BF16)
            w3b[...] = w3_ref[0].astype(BF16)
            w2b[...] = w2_ref[0].astype(BF16)

        xp = xs_ref[...]
        rows = lax.broadcasted_iota(jnp.int32, xp.shape, 0)
        mine = (rows >= lo_ref[j]) & (rows < hi_ref[j])
        xb = _unpack_pairs(jnp.where(mine, xp, jnp.uint32(0))).astype(BF16)
        h1 = jnp.dot(xb, w1b[...], preferred_element_type=F32)
        h3 = jnp.dot(xb, w3b[...], preferred_element_type=F32)
        hid = (h1 * _sigmoid(h1) * h3).astype(BF16)
        yp = _pack_pairs(jnp.dot(hid, w2b[...], preferred_element_type=F32))

        @pl.when(new_block)
        def _():
            ys_ref[...] = yp

        @pl.when(jnp.logical_not(new_block))
        def _():
            ys_ref[...] = jnp.where(mine, yp, ys_ref[...])


def _experts(wb, we, lo, hi, nw, xs, w1, w3, w2, *, layer):
    n_slots, W = xs.shape
    D, F = w1.shape[2:]
    blk = lambda j, wb, we, lo, hi, nw: (wb[j], 0)
    wsel = lambda j, wb, we, lo, hi, nw: (layer, we[j], 0, 0)
    grid_spec = pltpu.PrefetchScalarGridSpec(
        num_scalar_prefetch=5, grid=(wb.shape[0],),
        in_specs=[pl.BlockSpec((MOE_BM, W), blk), pl.BlockSpec((None, 1, D, F), wsel),
                  pl.BlockSpec((None, 1, D, F), wsel), pl.BlockSpec((None, 1, F, D), wsel)],
        out_specs=pl.BlockSpec((MOE_BM, W), blk),
        scratch_shapes=[pltpu.VMEM((D, F), BF16), pltpu.VMEM((D, F), BF16), pltpu.VMEM((F, D), BF16)])
    return pl.pallas_call(
        _expert_kernel, out_shape=jax.ShapeDtypeStruct((n_slots, W), jnp.uint32), grid_spec=grid_spec,
        compiler_params=_cparams(("arbitrary",)), name="moe_experts",
    )(wb, we, lo, hi, nw, xs, w1, w3, w2)


def _combine_kernel(dcur_ref, dnext_ref, w_ref, xs1_ref, g2_ref, fg_ref, ys_ref, o_ref, buf, sems, *, final):
    tm = xs1_ref.shape[0]
    groups = tm // SUBLANES
    i = pl.program_id(0)
    n = pl.num_programs(0)

    def issue(dref, slot):
        def body(g, carry):
            for r in range(SUBLANES):
                for k in range(TOP_K):
                    d = dref[g * (SUBLANES * TOP_K) + r * TOP_K + k]
                    pltpu.make_async_copy(ys_ref.at[d >> SUBLANE_BITS, pl.ds(d & (SUBLANES - 1), 1)],
                                          buf.at[slot, k, g, pl.ds(r, 1)], sems.at[slot]).start(priority=k % 2)
            return carry

        lax.fori_loop(0, groups, body, 0)

    def finish(slot):
        for k in range(TOP_K):
            pltpu.make_async_copy(ys_ref.at[pl.ds(0, groups)], buf.at[slot, k], sems.at[slot]).wait()
        w = w_ref[...]
        r = None
        for k in range(TOP_K):
            term = w[:, k:k + 1] * _unpack_pairs(buf[slot, k].reshape(tm, buf.shape[-1]))
            r = term if r is None else r + term
        x2 = xs1_ref[...] + g2_ref[0] * r
        if final:
            x2 = x2 * lax.rsqrt(jnp.mean(x2 * x2, axis=-1, keepdims=True) + EPS) * fg_ref[...]
        o_ref[...] = x2

    @pl.when(i == 0)
    def _():
        issue(dcur_ref, 0)

    for slot in range(2):
        @pl.when(jnp.logical_and(i + 1 < n, (i + 1) % 2 == slot))
        def _():
            issue(dnext_ref, slot)

    for slot in range(2):
        @pl.when(i % 2 == slot)
        def _():
            finish(slot)


def _combine(dest_flat, w_tok, xs1, g2, fg, ys, *, seq, final):
    N, D = xs1.shape
    n_slots, W = ys.shape
    tm = _tile(seq, 128)
    per_b = seq // tm
    n = N // tm
    return pl.pallas_call(
        functools.partial(_combine_kernel, final=final),
        out_shape=jax.ShapeDtypeStruct((N, D), F32),
        grid=(n,),
        in_specs=[pl.BlockSpec((tm * TOP_K,), lambda i: (i,), memory_space=pltpu.SMEM),
                  pl.BlockSpec((tm * TOP_K,), lambda i: (jnp.minimum(i + 1, n - 1),), memory_space=pltpu.SMEM),
                  pl.BlockSpec((tm, TOP_K), lambda i: (i, 0)),
                  pl.BlockSpec((tm, D), lambda i: (i, 0)),
                  pl.BlockSpec((1, 1, D), lambda i: (i // per_b, 0, 0)),
                  pl.BlockSpec((1, D), lambda i: (0, 0)),
                  pl.BlockSpec(memory_space=pl.ANY)],
        out_specs=pl.BlockSpec((tm, D), lambda i: (i, 0)),
        scratch_shapes=[pltpu.VMEM((2, TOP_K, tm // SUBLANES, SUBLANES, W), jnp.uint32), pltpu.SemaphoreType.DMA((2,))],
        compiler_params=_cparams(("arbitrary",)),
        name="moe_combine",
    )(dest_flat, dest_flat, w_tok, xs1, g2, fg, ys.reshape(n_slots // SUBLANES, SUBLANES, W))


def _rope_tables(S):
    half = HEAD_DIM // 2
    inv = ROPE_THETA ** (-jnp.arange(0, half, 2, dtype=F32) / half)
    t = jnp.arange(S, dtype=jnp.int32)

    def part(pos):
        ang = pos.astype(F32)[:, None] * inv[None, :]
        c, s, z = jnp.cos(ang), jnp.sin(ang), jnp.zeros_like(ang)
        return jnp.concatenate([c, c], -1), jnp.concatenate([-s, z], -1), jnp.concatenate([z, s], -1)

    row, col, lin = part(t // GRID_W), part(t % GRID_W), part(t)
    tabs_a = tuple(jnp.tile(jnp.concatenate([r, c], -1), (1, LANES // HEAD_DIM)) for r, c in zip(row, col))
    tabs_c = tuple(jnp.pad(v, ((0, 0), (0, LANES - C_ROPE))) for v in lin)
    return tabs_a, tabs_c


def _head_slabs(w, heads, groups):
    d = w.shape[0]
    hot = (jnp.arange(heads)[:, None] // (heads // groups) == jnp.arange(groups)[None, :]).astype(w.dtype)
    return (w.reshape(d, heads, 1, HEAD_DIM) * hot[None, :, :, None]).reshape(d, heads * LANES)


def _pad_w_in(w):
    d = w.shape[0]
    a, b, c = w[:, :A_IN], w[:, A_IN:A_IN + B_IN], w[:, A_IN + B_IN:]
    kr = jnp.pad(c[:, C_Q_LORA + C_KV_LORA:], ((0, 0), (0, LANES - C_ROPE)))
    return jnp.concatenate([_head_slabs(a[:, :A_WIDTH], A_HEADS, A_KV_HEADS), a[:, A_WIDTH:],
                            _head_slabs(b[:, :B_WIDTH], B_HEADS, B_KV_HEADS), b[:, B_WIDTH:],
                            c[:, :C_Q_LORA + C_KV_LORA], kr], axis=1).astype(BF16)


def _pad_w_uq(w):
    r = w.shape[0]
    w = w.reshape(r, C_HEADS, C_NOPE + C_ROPE)
    hot = (jnp.arange(C_HEADS)[:, None] % 2 == jnp.arange(2)[None, :]).astype(w.dtype)
    nope = (w[:, :, None, :C_NOPE] * hot[None, :, :, None]).reshape(r, C_HEADS, LANES)
    rope = jnp.pad(w[:, :, C_NOPE:], ((0, 0), (0, 0), (0, LANES - C_ROPE)))
    return jnp.concatenate([nope, rope], axis=-1).reshape(r, C_HEADS * 2 * LANES).astype(BF16)


def _select_mats(heads, groups, out_w):
    h = jnp.arange(heads)[:, None, None]
    r = jnp.arange(LANES)[None, :, None]
    c = jnp.arange(out_w)[None, None, :]
    half = h // (heads // groups)
    return ((r // HEAD_DIM == half) & (c == HEAD_DIM * h + r % HEAD_DIM)).astype(BF16)


def kernel(x, c, mod_w, mod_b, norm1_g, norm2_g, w_in, a_qnorm_g, a_knorm_g, b_sink, c_qnorm_g, c_kvnorm_g, c_w_uq, c_w_uk, c_w_uv, out_norm_g, w_out, router_w, router_b, exp_w1, exp_w3, exp_w2, sh_w1, sh_w3, sh_w2, final_g):
    B, S, D = x.shape
    L = mod_w.shape[0]
    E = router_w.shape[2]
    N = B * S
    n_slots = N * TOP_K
    assert n_slots % MOE_BM == 0
    n_work = n_slots // MOE_BM + E

    mod = _mod(c, mod_w, mod_b).reshape(L, B, 6, 1, D)
    tabs_a, tabs_c = _rope_tables(S)
    esel_a = _select_mats(A_HEADS, A_KV_HEADS, A_WIDTH)
    esel_c = _select_mats(2, 2, LANES)
    slopes = (jnp.exp2(-8.0 * jnp.arange(1, B_HEADS + 1, dtype=F32) / B_HEADS) * LOG2E).reshape(B_HEADS, 1, 1)

    for l in range(L):
        sh1, sc1, g1, sh2, sc2, g2 = [mod[l, :, i] for i in range(6)]
        gq2 = jnp.tile(a_qnorm_g[l], LANES // HEAD_DIM).reshape(1, LANES)
        gk2 = jnp.tile(a_knorm_g[l], LANES // HEAD_DIM).reshape(1, LANES)
        qa, ka, va, qb, kb, vb, qc, kc, vc = _inproj(
            x, sc1, sh1, norm1_g[l].reshape(1, D), _pad_w_in(w_in[l]), tabs_a, tabs_c, gq2, gk2,
            c_qnorm_g[l].reshape(1, -1), c_kvnorm_g[l].reshape(1, -1), _pad_w_uq(c_w_uq[l]),
            c_w_uk[l].astype(BF16), c_w_uv[l].astype(BF16))
        oa = _flash(qa[:, None], ka[:, None], va[:, None], esel_a, tq=1024, tk=512, cw=512, unroll=1)
        ob = _window(qb, kb, vb, esel_a, slopes, (b_sink[l].astype(F32) * LOG2E).reshape(B_HEADS, 1, 1), tq=256)
        oc = _flash(qc.reshape(B, C_HEADS // 2, 2, S, 2 * LANES), kc, vc, esel_c, tq=2048, tk=512, cw=512, unroll=1)
        xs1, u2, scores_t = _post(
            oa, ob, oc, x, g1, sc2, sh2, g2, out_norm_g[l].reshape(1, -1), w_out[l].astype(BF16),
            norm2_g[l].reshape(1, D), router_w[l].T.astype(BF16), sh_w1[l].astype(BF16), sh_w3[l].astype(BF16),
            sh_w2[l].astype(BF16))
        idx, w_top, rank, cnt = _route(scores_t, router_b[l])
        counts = cnt[:, 0].astype(jnp.int32)
        ends = jnp.cumsum(counts)
        starts = ends - counts
        dest = _slots(idx, rank, starts).T.reshape(N * TOP_K)
        first_blk = starts // MOE_BM
        n_items = jnp.where(counts > 0, (ends - 1) // MOE_BM - first_blk + 1, 0)
        item_ends = jnp.cumsum(n_items)
        nw = item_ends[-1].reshape(1).astype(jnp.int32)
        wid = jnp.minimum(jnp.arange(n_work, dtype=jnp.int32), nw[0] - 1)
        we = jnp.minimum(jnp.searchsorted(item_ends, wid, side='right'), E - 1).astype(jnp.int32)
        wb = (first_blk[we] + wid - (item_ends - n_items)[we]).astype(jnp.int32)
        lo = jnp.clip(starts[we] - wb * MOE_BM, 0, MOE_BM).astype(jnp.int32)
        hi = jnp.clip(ends[we] - wb * MOE_BM, 0, MOE_BM).astype(jnp.int32)
        xs = _dispatch(dest, u2, n_slots)
        ys = _experts(wb, we, lo, hi, nw, xs, exp_w1, exp_w3, exp_w2, layer=l)
        x = _combine(dest, w_top.T, xs1.reshape(N, D), g2, final_g.reshape(1, D), ys, seq=S,
                     final=(l == L - 1)).reshape(B, S, D)
    return x
```

```python
import functools

import jax
import jax.numpy as jnp
from jax import lax
from jax.experimental import pallas as pl
from jax.experimental.pallas import tpu as pltpu

F32 = jnp.float32
BF16 = jnp.bfloat16

HEAD_DIM = 64
GRID_W = 64
ROPE_THETA = 10000.0
EPS = 1e-6
NEG_INF = -1e30
A_HEADS, A_KV_HEADS = 6, 2
B_HEADS, B_KV_HEADS = 6, 2
WINDOW = 128
C_HEADS, C_Q_LORA, C_KV_LORA, C_NOPE, C_ROPE, C_V = 4, 256, 128, 64, 32, 64
A_WIDTH = A_HEADS * HEAD_DIM
B_WIDTH = B_HEADS * HEAD_DIM
C_WIDTH = C_HEADS * C_V
A_IN = (A_HEADS + 2 * A_KV_HEADS) * HEAD_DIM
B_IN = (B_HEADS + 2 * B_KV_HEADS) * HEAD_DIM
TOP_K = 8
N_GROUPS = 8
TOPK_GROUPS = 4
ROUTED_SCALE = 2.5
LOG2E = 1.4426950408889634

LANES = 128
SUBLANE_BITS = 3
SUBLANES = 1 << SUBLANE_BITS
OFF_AQ = 0
OFF_AK = OFF_AQ + A_HEADS * LANES
OFF_AV = OFF_AK + LANES
OFF_BQ = OFF_AV + LANES
OFF_BK = OFF_BQ + B_HEADS * LANES
OFF_BV = OFF_BK + LANES
OFF_CQ = OFF_BV + LANES
OFF_CKV = OFF_CQ + C_Q_LORA
OFF_CKR = OFF_CKV + C_KV_LORA
IN_PAD = OFF_CKR + LANES

ONES_ROWS = 16
VT_ROWS = LANES + ONES_ROWS
MOE_BM = 256
VMEM_LIMIT = 56 * 1024 * 1024


def _tile(n, pref):
    t = min(n, pref)
    assert n % t == 0, (n, t)
    return t


def _cparams(sem, vmem=VMEM_LIMIT):
    return pltpu.CompilerParams(dimension_semantics=sem, vmem_limit_bytes=vmem)


def _mod_kernel(c_ref, w_ref, b_ref, o_ref):
    c = c_ref[...]
    ca = c * (1.0 / (1.0 + jnp.exp(-c)))
    o_ref[0] = jnp.dot(ca, w_ref[0], preferred_element_type=F32, precision=lax.Precision.HIGHEST) + b_ref[0]


def _mod(c, mod_w, mod_b):
    L, D, W = mod_w.shape
    B = c.shape[0]
    tn = _tile(W, 1536)
    return pl.pallas_call(
        _mod_kernel,
        out_shape=jax.ShapeDtypeStruct((L, B, W), F32),
        grid=(L, W // tn),
        in_specs=[
            pl.BlockSpec((B, D), lambda l, j: (0, 0)),
            pl.BlockSpec((1, D, tn), lambda l, j: (l, 0, j)),
            pl.BlockSpec((1, 1, tn), lambda l, j: (l, 0, j)),
        ],
        out_specs=pl.BlockSpec((1, B, tn), lambda l, j: (l, 0, j)),
        compiler_params=_cparams(("arbitrary", "arbitrary")),
        name="mod_proj",
    )(c, mod_w, mod_b.reshape(L, 1, W))


def _rope(v, c, sa, sb):
    return v * c + pltpu.roll(v, LANES - 16, 1) * sa + pltpu.roll(v, 16, 1) * sb


def _inproj_kernel(x_ref, sc_ref, sh_ref, g_ref, w_ref, ca_ref, saa_ref, sba_ref, cc_ref, sac_ref, sbc_ref,
                   gq_ref, gk_ref, gcq_ref, gckv_ref, wuq_ref, wuk_ref, wuv_ref,
                   qa_ref, ka_ref, va_ref, qb_ref, kb_ref, vb_ref, qc_ref, kc_ref, vc_ref, *, qs_ab, qs_c):
    x = x_ref[0]
    u = x * lax.rsqrt(jnp.mean(x * x, axis=-1, keepdims=True) + EPS) * g_ref[...]
    u = u * (1.0 + sc_ref[0]) + sh_ref[0]
    p = jnp.dot(u.astype(BF16), w_ref[...], preferred_element_type=F32)
    ca, saa, sba = ca_ref[...], saa_ref[...], sba_ref[...]
    cc, sac, sbc = cc_ref[...], sac_ref[...], sbc_ref[...]

    gq = gq_ref[...]
    for h in range(A_HEADS):
        v = p[:, OFF_AQ + LANES * h:OFF_AQ + LANES * (h + 1)]
        ss = jnp.sum(v * v, axis=-1, keepdims=True) * (1.0 / HEAD_DIM)
        v = v * lax.rsqrt(ss + EPS) * gq
        qa_ref[0, h] = (_rope(v, ca, saa, sba) * qs_ab).astype(BF16)
    k = p[:, OFF_AK:OFF_AK + LANES]
    lane = lax.broadcasted_iota(jnp.int32, k.shape, 1)
    k2 = k * k
    s0 = jnp.sum(jnp.where(lane < HEAD_DIM, k2, 0.0), axis=-1, keepdims=True)
    s1 = jnp.sum(k2, axis=-1, keepdims=True) - s0
    r = jnp.where(lane < HEAD_DIM, lax.rsqrt(s0 * (1.0 / HEAD_DIM) + EPS), lax.rsqrt(s1 * (1.0 / HEAD_DIM) + EPS))
    ka_ref[0] = _rope(k * r * gk_ref[...], ca, saa, sba).astype(BF16)
    ones = jnp.ones((ONES_ROWS, k.shape[0]), BF16)
    va_ref[0] = jnp.concatenate([p[:, OFF_AV:OFF_AV + LANES].T.astype(BF16), ones], axis=0)

    for h in range(B_HEADS):
        qb_ref[0, h] = (p[:, OFF_BQ + LANES * h:OFF_BQ + LANES * (h + 1)] * qs_ab).astype(BF16)
    kb_ref[0] = p[:, OFF_BK:OFF_BK + LANES].astype(BF16)
    vb_ref[0] = p[:, OFF_BV:OFF_BV + LANES].astype(BF16)

    cq = p[:, OFF_CQ:OFF_CQ + C_Q_LORA]
    cq = cq * lax.rsqrt(jnp.mean(cq * cq, axis=-1, keepdims=True) + EPS) * gcq_ref[...]
    qh = jnp.dot(cq.astype(BF16), wuq_ref[...], preferred_element_type=F32)
    for h in range(C_HEADS):
        base = 2 * LANES * h
        qc_ref[0, h, :, 0:LANES] = (qh[:, base:base + LANES] * qs_c).astype(BF16)
        qc_ref[0, h, :, LANES:2 * LANES] = (_rope(qh[:, base + LANES:base + 2 * LANES], cc, sac, sbc) * qs_c).astype(BF16)
    ckv = p[:, OFF_CKV:OFF_CKV + C_KV_LORA]
    ckv = (ckv * lax.rsqrt(jnp.mean(ckv * ckv, axis=-1, keepdims=True) + EPS) * gckv_ref[...]).astype(BF16)
    kn = jnp.dot(ckv, wuk_ref[...], preferred_element_type=F32)
    vv = jnp.dot(ckv, wuv_ref[...], preferred_element_type=F32)
    kr = _rope(p[:, OFF_CKR:OFF_CKR + LANES], cc, sac, sbc).astype(BF16)
    for pr in range(C_HEADS // 2):
        kc_ref[0, pr, :, 0:LANES] = kn[:, LANES * pr:LANES * (pr + 1)].astype(BF16)
        kc_ref[0, pr, :, LANES:2 * LANES] = kr
        vc_ref[0, pr] = jnp.concatenate([vv[:, LANES * pr:LANES * (pr + 1)].T.astype(BF16), ones], axis=0)


def _inproj(x, sc1, sh1, g, w_pad, tabs_a, tabs_c, gq, gk, gcq, gckv, wuq_pad, wuk, wuv):
    B, S, D = x.shape
    tm = _tile(S, 512)
    full = lambda shp: pl.BlockSpec(shp, lambda b, i: (0,) * len(shp))
    tab = pl.BlockSpec((tm, LANES), lambda b, i: (i, 0))
    mod = pl.BlockSpec((1, 1, D), lambda b, i: (b, 0, 0))
    kern = functools.partial(_inproj_kernel, qs_ab=HEAD_DIM ** -0.5 * LOG2E, qs_c=(C_NOPE + C_ROPE) ** -0.5 * LOG2E)
    outs = (
        jax.ShapeDtypeStruct((B, A_HEADS, S, LANES), BF16), jax.ShapeDtypeStruct((B, S, LANES), BF16),
        jax.ShapeDtypeStruct((B, VT_ROWS, S), BF16),
        jax.ShapeDtypeStruct((B, B_HEADS, S, LANES), BF16), jax.ShapeDtypeStruct((B, S, LANES), BF16),
        jax.ShapeDtypeStruct((B, S, LANES), BF16),
        jax.ShapeDtypeStruct((B, C_HEADS, S, 2 * LANES), BF16), jax.ShapeDtypeStruct((B, C_HEADS // 2, S, 2 * LANES), BF16),
        jax.ShapeDtypeStruct((B, C_HEADS // 2, VT_ROWS, S), BF16),
    )
    slab = pl.BlockSpec((1, tm, LANES), lambda b, i: (b, i, 0))
    out_specs = (
        pl.BlockSpec((1, A_HEADS, tm, LANES), lambda b, i: (b, 0, i, 0)), slab,
        pl.BlockSpec((1, VT_ROWS, tm), lambda b, i: (b, 0, i)),
        pl.BlockSpec((1, B_HEADS, tm, LANES), lambda b, i: (b, 0, i, 0)), slab, slab,
        pl.BlockSpec((1, C_HEADS, tm, 2 * LANES), lambda b, i: (b, 0, i, 0)),
        pl.BlockSpec((1, C_HEADS // 2, tm, 2 * LANES), lambda b, i: (b, 0, i, 0)),
        pl.BlockSpec((1, C_HEADS // 2, VT_ROWS, tm), lambda b, i: (b, 0, 0, i)),
    )
    return pl.pallas_call(
        kern, out_shape=outs, grid=(B, S // tm),
        in_specs=[pl.BlockSpec((1, tm, D), lambda b, i: (b, i, 0)), mod, mod, full((1, D)), full((D, IN_PAD)),
                  tab, tab, tab, tab, tab, tab,
                  full((1, LANES)), full((1, LANES)), full((1, C_Q_LORA)), full((1, C_KV_LORA)),
                  full(wuq_pad.shape), full(wuk.shape), full(wuv.shape)],
        out_specs=out_specs,
        compiler_params=_cparams(("arbitrary", "arbitrary")),
        name="norm_inproj",
    )(x, sc1, sh1, g, w_pad, *tabs_a, *tabs_c, gq, gk, gcq, gckv, wuq_pad, wuk, wuv)


def _flash_kernel(q_ref, k_ref, vt_ref, e_ref, o_ref, qt_ref, m_ref, acc_ref, *, tk, cw, unroll):
    H, tq, dk = q_ref.shape[2:]
    S = k_ref.shape[2]
    R = H * tq
    for h in range(H):
        qt_ref[:, h * tq:(h + 1) * tq] = q_ref[0, 0, h].astype(F32).T.astype(BF16)
    m_ref[...] = jnp.full(m_ref.shape, NEG_INF, F32)
    acc_ref[...] = jnp.zeros(acc_ref.shape, F32)

    def body(j, carry):
        off = pl.multiple_of(j * tk, tk)
        k = k_ref[0, 0, pl.ds(off, tk), :]
        vt = vt_ref[0, 0, :, pl.ds(off, tk)]
        nc = R // cw
        scores = lambda c: jnp.dot(k, qt_ref[:, c * cw:(c + 1) * cw], preferred_element_type=F32)
        accs, ms = [], []
        s_next = scores(0)
        for c in range(nc):
            sl = slice(c * cw, (c + 1) * cw)
            s = s_next
            if c + 1 < nc:
                s_next = scores(c + 1)
            m_prev = m_ref[:, sl]
            m_new = jnp.maximum(m_prev, jnp.max(s, axis=0, keepdims=True))
            p = jnp.exp2(s - m_new).astype(BF16)
            alpha = jnp.exp2(m_prev - m_new)
            accs.append(alpha * acc_ref[:, sl] + jnp.dot(vt, p, preferred_element_type=F32))
            ms.append(m_new)
        acc_ref[...] = jnp.concatenate(accs, axis=1)
        m_ref[...] = jnp.concatenate(ms, axis=1)
        return carry

    lax.fori_loop(0, S // tk, body, 0, unroll=unroll)
    o_t = acc_ref[0:LANES, :] / acc_ref[LANES:LANES + 1, :]
    out = None
    for h in range(H):
        o_h = o_t[:, h * tq:(h + 1) * tq].T.astype(BF16)
        term = jnp.dot(o_h, e_ref[h], preferred_element_type=F32)
        out = term if out is None else out + term
    o_ref[0] = out.astype(BF16)


def _flash(q, k, vt, esel, *, tq, tk, cw=256, unroll=2):
    B, P, H, S, dk = q.shape
    W = esel.shape[2]
    tq = _tile(S, tq)
    tk = _tile(S, tk)
    R = H * tq
    cw = _tile(R, cw)
    return pl.pallas_call(
        functools.partial(_flash_kernel, tk=tk, cw=cw, unroll=unroll),
        out_shape=jax.ShapeDtypeStruct((B, S, P * W), BF16),
        grid=(B, P, S // tq),
        in_specs=[pl.BlockSpec((1, 1, H, tq, dk), lambda b, p, i: (b, p, 0, i, 0)),
                  pl.BlockSpec((1, 1, S, dk), lambda b, p, i: (b, p, 0, 0)),
                  pl.BlockSpec((1, 1, VT_ROWS, S), lambda b, p, i: (b, p, 0, 0)),
                  pl.BlockSpec((H, LANES, W), lambda b, p, i: (0, 0, 0))],
        out_specs=pl.BlockSpec((1, tq, W), lambda b, p, i: (b, i, p)),
        scratch_shapes=[pltpu.VMEM((dk, R), BF16), pltpu.VMEM((1, R), F32), pltpu.VMEM((VT_ROWS, R), F32)],
        compiler_params=_cparams(("arbitrary", "arbitrary", "arbitrary")),
        name="flash_attn",
    )(q, k, vt, esel)


def _window_kernel(q_ref, k_ref, v_ref, e_ref, slope_ref, sink_ref, o_ref, *, tq):
    H = q_ref.shape[1]
    S = k_ref.shape[1]
    wk = min(S, tq + 2 * WINDOW)
    i = pl.program_id(1)
    start = pl.multiple_of(jnp.clip(i * tq - WINDOW, 0, S - wk), WINDOW)
    q = q_ref[0].reshape(H * tq, LANES)
    k = k_ref[0, pl.ds(start, wk), :]
    v = v_ref[0, pl.ds(start, wk), :]
    s = lax.dot_general(q, k, (((1,), (1,)), ((), ())), preferred_element_type=F32).reshape(H, tq, wk)
    tpos = i * tq + lax.broadcasted_iota(jnp.int32, (tq, wk), 0)
    spos = start + lax.broadcasted_iota(jnp.int32, (tq, wk), 1)
    dist = jnp.abs(spos - tpos)
    s = s - slope_ref[...] * dist.astype(F32)[None]
    s = jnp.where((dist <= WINDOW)[None], s, NEG_INF)
    sink = sink_ref[...]
    m = jnp.maximum(jnp.max(s, axis=-1, keepdims=True), sink)
    e = jnp.exp2(s - m)
    den = jnp.sum(e, axis=-1, keepdims=True) + jnp.exp2(sink - m)
    o = jnp.dot(e.reshape(H * tq, wk).astype(BF16), v, preferred_element_type=F32).reshape(H, tq, LANES)
    o = (o / den).astype(BF16)
    out = jnp.dot(o[0], e_ref[0], preferred_element_type=F32)
    for h in range(1, H):
        out = out + jnp.dot(o[h], e_ref[h], preferred_element_type=F32)
    o_ref[0] = out.astype(BF16)


def _window(q, k, v, esel, slopes, sink, *, tq):
    B, H, S, _ = q.shape
    W = esel.shape[2]
    tq = _tile(S, tq)
    return pl.pallas_call(
        functools.partial(_window_kernel, tq=tq),
        out_shape=jax.ShapeDtypeStruct((B, S, W), BF16),
        grid=(B, S // tq),
        in_specs=[pl.BlockSpec((1, H, tq, LANES), lambda b, i: (b, 0, i, 0)),
                  pl.BlockSpec((1, S, LANES), lambda b, i: (b, 0, 0)),
                  pl.BlockSpec((1, S, LANES), lambda b, i: (b, 0, 0)),
                  pl.BlockSpec((H, LANES, W), lambda b, i: (0, 0, 0)),
                  pl.BlockSpec((H, 1, 1), lambda b, i: (0, 0, 0)),
                  pl.BlockSpec((H, 1, 1), lambda b, i: (0, 0, 0))],
        out_specs=pl.BlockSpec((1, tq, W), lambda b, i: (b, i, 0)),
        compiler_params=_cparams(("arbitrary", "arbitrary")),
        name="window_attn",
    )(q, k, v, esel, slopes, sink)


def _sigmoid(v):
    return 1.0 / (1.0 + jnp.exp(-v))


def _store_token_tiles(ref, x):
    ch = x.shape[1] // LANES
    for c in range(ch):
        ref[pl.ds(c, x.shape[0], stride=ch), :] = x[:, LANES * c:LANES * (c + 1)]


def _load_token_tiles(ref, n, ch):
    return [ref[pl.ds(c, n, stride=ch), :] for c in range(ch)]


def _post_kernel(oa_ref, ob_ref, oc_ref, x_ref, g1_ref, sc2_ref, sh2_ref, g2_ref, og_ref, wout_ref, n2_ref, rwt_ref,
                 sw1_ref, sw3_ref, sw2_ref, xs1_ref, u2_ref, st_ref):
    og = og_ref[...]

    def rn(o_ref, g):
        o = o_ref[0].astype(F32)
        return (o * lax.rsqrt(jnp.mean(o * o, axis=-1, keepdims=True) + EPS) * g).astype(BF16)

    mixed = jnp.concatenate([rn(oa_ref, og[:, :A_WIDTH]), rn(ob_ref, og[:, A_WIDTH:A_WIDTH + B_WIDTH]),
                             rn(oc_ref, og[:, A_WIDTH + B_WIDTH:])], axis=-1)
    x1 = x_ref[0] + g1_ref[0] * jnp.dot(mixed, wout_ref[...], preferred_element_type=F32)
    u2 = x1 * lax.rsqrt(jnp.mean(x1 * x1, axis=-1, keepdims=True) + EPS) * n2_ref[...]
    u2 = u2 * (1.0 + sc2_ref[0]) + sh2_ref[0]
    _store_token_tiles(u2_ref, u2)
    ub = u2.astype(BF16)
    logits_t = lax.dot_general(rwt_ref[...], ub, (((1,), (1,)), ((), ())), preferred_element_type=F32)
    st_ref[...] = _sigmoid(logits_t)
    h1 = jnp.dot(ub, sw1_ref[...], preferred_element_type=F32)
    h3 = jnp.dot(ub, sw3_ref[...], preferred_element_type=F32)
    hid = (h1 * _sigmoid(h1) * h3).astype(BF16)
    xs1_ref[0] = x1 + g2_ref[0] * jnp.dot(hid, sw2_ref[...], preferred_element_type=F32)


def _post(oa, ob, oc, x, g1, sc2, sh2, g2, og, wout, n2, rwt, sw1, sw3, sw2):
    B, S, D = x.shape
    E = rwt.shape[0]
    tm = _tile(S, 512)
    nt = S // tm
    full = lambda a: pl.BlockSpec(a.shape, lambda b, i: (0,) * a.ndim)
    mod = pl.BlockSpec((1, 1, D), lambda b, i: (b, 0, 0))
    row = lambda w: pl.BlockSpec((1, tm, w), lambda b, i: (b, i, 0))
    return pl.pallas_call(
        _post_kernel,
        out_shape=(jax.ShapeDtypeStruct((B, S, D), F32), jax.ShapeDtypeStruct((B * S * D // LANES, LANES), F32),
                   jax.ShapeDtypeStruct((E, B * S), F32)),
        grid=(B, nt),
        in_specs=[row(A_WIDTH), row(B_WIDTH), row(C_WIDTH), row(D), mod, mod, mod, mod, full(og), full(wout), full(n2),
                  full(rwt), full(sw1), full(sw3), full(sw2)],
        out_specs=(row(D), pl.BlockSpec((tm * D // LANES, LANES), lambda b, i: (b * nt + i, 0)),
                   pl.BlockSpec((E, tm), lambda b, i: (0, b * nt + i))),
        compiler_params=_cparams(("arbitrary", "arbitrary")),
        name="post_attn",
    )(oa, ob, oc, x, g1, sc2, sh2, g2, og, wout, n2, rwt, sw1, sw3, sw2)


def _route_kernel(st_ref, rb_ref, tri_ref, idx_ref, w_ref, rank_ref, cnt_ref, carry_ref):
    E, tm = st_ref.shape
    per = E // N_GROUPS
    big = float(E)

    @pl.when(pl.program_id(0) == 0)
    def _():
        carry_ref[...] = jnp.zeros(carry_ref.shape, F32)

    s = st_ref[...]
    choice = s + rb_ref[...]
    c3 = choice.reshape(N_GROUPS, per, tm)
    io3 = lax.broadcasted_iota(jnp.int32, c3.shape, 1).astype(F32)
    m1 = jnp.max(c3, axis=1, keepdims=True)
    i1 = jnp.min(jnp.where(c3 == m1, io3, big), axis=1, keepdims=True)
    m2 = jnp.max(jnp.where(io3 == i1, -jnp.inf, c3), axis=1, keepdims=True)
    gs = (m1 + m2).reshape(N_GROUPS, tm)
    gio = lax.broadcasted_iota(jnp.int32, gs.shape, 0).astype(F32)
    keep = jnp.zeros(gs.shape, F32)
    for _ in range(TOPK_GROUPS):
        mg = jnp.max(gs, axis=0, keepdims=True)
        ig = jnp.min(jnp.where(gs == mg, gio, big), axis=0, keepdims=True)
        sel = gio == ig
        keep = jnp.where(sel, 1.0, keep)
        gs = jnp.where(sel, -jnp.inf, gs)
    cur = jnp.where(keep.reshape(N_GROUPS, 1, tm) > 0.5, c3, NEG_INF).reshape(E, tm)
    eio = lax.broadcasted_iota(jnp.int32, (E, tm), 0).astype(F32)
    sels, idxs, ws = [], [], []
    for _ in range(TOP_K):
        m = jnp.max(cur, axis=0, keepdims=True)
        ik = jnp.min(jnp.where(cur == m, eio, big), axis=0, keepdims=True)
        sel = eio == ik
        ws.append(jnp.sum(jnp.where(sel, s, 0.0), axis=0, keepdims=True))
        cur = jnp.where(sel, -jnp.inf, cur)
        sels.append(sel)
        idxs.append(ik)
    wsum = ws[0]
    for wk in ws[1:]:
        wsum = wsum + wk
    w_ref[...] = jnp.concatenate([wk / wsum * ROUTED_SCALE for wk in ws], axis=0)
    idx_ref[...] = jnp.concatenate(idxs, axis=0).astype(jnp.int32)
    onehot = jnp.zeros((E, tm), F32)
    for sel in sels:
        onehot = jnp.where(sel, 1.0, onehot)
    prefix = jnp.dot(onehot.astype(BF16), tri_ref[...], preferred_element_type=F32) + carry_ref[...]
    rank_ref[...] = jnp.concatenate(
        [jnp.sum(jnp.where(sel, prefix, 0.0), axis=0, keepdims=True) for sel in sels], axis=0).astype(jnp.int32)
    carry = carry_ref[...] + jnp.sum(onehot, axis=1, keepdims=True)
    carry_ref[...] = carry
    cnt_ref[...] = jnp.broadcast_to(carry, cnt_ref.shape)


def _route(scores_t, router_b):
    E, N = scores_t.shape
    tm = _tile(N, 512)
    tri = (jnp.arange(tm)[:, None] < jnp.arange(tm)[None, :]).astype(BF16)
    return pl.pallas_call(
        _route_kernel,
        out_shape=(jax.ShapeDtypeStruct((TOP_K, N), jnp.int32), jax.ShapeDtypeStruct((TOP_K, N), F32),
                   jax.ShapeDtypeStruct((TOP_K, N), jnp.int32), jax.ShapeDtypeStruct((E, LANES), F32)),
        grid=(N // tm,),
        in_specs=[pl.BlockSpec((E, tm), lambda i: (0, i)), pl.BlockSpec((E, 1), lambda i: (0, 0)),
                  pl.BlockSpec((tm, tm), lambda i: (0, 0))],
        out_specs=(pl.BlockSpec((TOP_K, tm), lambda i: (0, i)), pl.BlockSpec((TOP_K, tm), lambda i: (0, i)),
                   pl.BlockSpec((TOP_K, tm), lambda i: (0, i)), pl.BlockSpec((E, LANES), lambda i: (0, 0))),
        scratch_shapes=[pltpu.VMEM((E, 1), F32)],
        compiler_params=_cparams(("arbitrary",)),
        name="route_topk",
    )(scores_t, router_b.reshape(E, 1), tri)


def _slots_kernel(idx_ref, rank_ref, starts_ref, dest_ref):
    E = starts_ref.shape[0]
    tm = idx_ref.shape[1]
    eio = lax.broadcasted_iota(jnp.int32, (E, tm), 0)
    starts = starts_ref[...]
    rows = [jnp.sum(jnp.where(eio == idx_ref[k:k + 1, :], starts, 0.0), axis=0, keepdims=True) for k in range(TOP_K)]
    dest_ref[...] = jnp.concatenate(rows, axis=0).astype(jnp.int32) + rank_ref[...]


def _slots(idx, rank, starts):
    K, N = idx.shape
    E = starts.shape[0]
    tm = _tile(N, 512)
    blk = pl.BlockSpec((K, tm), lambda i: (0, i))
    return pl.pallas_call(
        _slots_kernel, out_shape=jax.ShapeDtypeStruct((K, N), jnp.int32), grid=(N // tm,),
        in_specs=[blk, blk, pl.BlockSpec((E, 1), lambda i: (0, 0))], out_specs=blk,
        compiler_params=_cparams(("arbitrary",)), name="moe_slots",
    )(idx, rank, starts.astype(F32).reshape(E, 1))


TOKENS_PER_ISSUE = 8


def _dispatch_kernel(dest_ref, u_ref, xs_ref, sem, *, ch):
    tm = u_ref.shape[0] // ch

    def body(g, carry):
        base = pl.multiple_of(g * (TOKENS_PER_ISSUE * ch), TOKENS_PER_ISSUE * ch)
        for r in range(TOKENS_PER_ISSUE):
            for k in range(TOP_K):
                d = dest_ref[g * (TOKENS_PER_ISSUE * TOP_K) + r * TOP_K + k]
                pltpu.make_async_copy(u_ref.at[pl.ds(base + r * ch, ch)],
                                      xs_ref.at[pl.ds(pl.multiple_of(d * ch, ch), ch)],
                                      sem).start(priority=k % 2)
        return carry

    lax.fori_loop(0, tm // TOKENS_PER_ISSUE, body, 0)
    for k in range(TOP_K):
        pltpu.make_async_copy(u_ref, xs_ref.at[pl.ds(0, tm * ch)], sem).wait()


def _dispatch(dest_flat, u2t, n_slots, ch):
    N = u2t.shape[0] // ch
    tm = _tile(N, 256)
    return pl.pallas_call(
        functools.partial(_dispatch_kernel, ch=ch),
        out_shape=jax.ShapeDtypeStruct((n_slots * ch, LANES), F32),
        grid=(N // tm,),
        in_specs=[pl.BlockSpec((tm * TOP_K,), lambda i: (i,), memory_space=pltpu.SMEM),
                  pl.BlockSpec((tm * ch, LANES), lambda i: (i, 0))],
        out_specs=pl.BlockSpec(memory_space=pl.ANY),
        scratch_shapes=[pltpu.SemaphoreType.DMA(())],
        compiler_params=pltpu.CompilerParams(dimension_semantics=("arbitrary",), vmem_limit_bytes=VMEM_LIMIT,
                                             has_side_effects=True),
        name="moe_dispatch",
    )(dest_flat, u2t)


def _expert_kernel(wb_ref, we_ref, lo_ref, hi_ref, nw_ref, xs_ref, w1_ref, w3_ref, w2_ref, ys_ref, w1b, w3b, w2b):
    j = pl.program_id(0)

    @pl.when(j < nw_ref[0])
    def _():
        prev = jnp.maximum(j - 1, 0)
        new_expert = jnp.logical_or(j == 0, we_ref[j] != we_ref[prev])
        new_block = jnp.logical_or(j == 0, wb_ref[j] != wb_ref[prev])

        @pl.when(new_expert)
        def _():
            w1b[...] = w1_ref[0].astype(BF16)
            w3b[...] = w3_ref[0].astype(BF16)
            w2b[...] = w2_ref[0].astype(BF16)

        ch = w1b.shape[0] // LANES
        rows = lax.broadcasted_iota(jnp.int32, (MOE_BM, LANES), 0)
        mine = (rows >= lo_ref[j]) & (rows < hi_ref[j])
        xb = jnp.concatenate([jnp.where(mine, xc, 0.0).astype(BF16) for xc in _load_token_tiles(xs_ref, MOE_BM, ch)],
                             axis=1)
        h1 = jnp.dot(xb, w1b[...], preferred_element_type=F32)
        h3 = jnp.dot(xb, w3b[...], preferred_element_type=F32)
        hid = (h1 * _sigmoid(h1) * h3).astype(BF16)
        y = jnp.dot(hid, w2b[...], preferred_element_type=F32)

        @pl.when(new_block)
        def _():
            _store_token_tiles(ys_ref, y)

        @pl.when(jnp.logical_not(new_block))
        def _():
            for c, prev in enumerate(_load_token_tiles(ys_ref, MOE_BM, ch)):
                ys_ref[pl.ds(c, MOE_BM, stride=ch), :] = jnp.where(mine, y[:, LANES * c:LANES * (c + 1)], prev)


def _experts(wb, we, lo, hi, nw, xs, w1, w3, w2, *, layer):
    D, F = w1.shape[2:]
    ch = D // LANES
    rows = MOE_BM * ch
    blk = lambda j, wb, we, lo, hi, nw: (wb[j], 0)
    wsel = lambda j, wb, we, lo, hi, nw: (layer, we[j], 0, 0)
    grid_spec = pltpu.PrefetchScalarGridSpec(
        num_scalar_prefetch=5, grid=(wb.shape[0],),
        in_specs=[pl.BlockSpec((rows, LANES), blk), pl.BlockSpec((None, 1, D, F), wsel),
                  pl.BlockSpec((None, 1, D, F), wsel), pl.BlockSpec((None, 1, F, D), wsel)],
        out_specs=pl.BlockSpec((rows, LANES), blk),
        scratch_shapes=[pltpu.VMEM((D, F), BF16), pltpu.VMEM((D, F), BF16), pltpu.VMEM((F, D), BF16)])
    return pl.pallas_call(
        _expert_kernel, out_shape=jax.ShapeDtypeStruct(xs.shape, F32), grid_spec=grid_spec,
        compiler_params=_cparams(("arbitrary",)), name="moe_experts",
    )(wb, we, lo, hi, nw, xs, w1, w3, w2)


def _combine_kernel(dcur_ref, dnext_ref, w_ref, xs1_ref, g2_ref, fg_ref, ys_ref, o_ref, buf, sems, *, final):
    tm, D = xs1_ref.shape
    ch = D // LANES
    i = pl.program_id(0)
    n = pl.num_programs(0)

    def issue(dref, slot):
        def body(g, carry):
            base = pl.multiple_of(g * (TOKENS_PER_ISSUE * ch), TOKENS_PER_ISSUE * ch)
            for r in range(TOKENS_PER_ISSUE):
                for k in range(TOP_K):
                    d = dref[g * (TOKENS_PER_ISSUE * TOP_K) + r * TOP_K + k]
                    pltpu.make_async_copy(ys_ref.at[pl.ds(pl.multiple_of(d * ch, ch), ch)],
                                          buf.at[slot, k, pl.ds(base + r * ch, ch)],
                                          sems.at[slot]).start(priority=k % 2)
            return carry

        lax.fori_loop(0, tm // TOKENS_PER_ISSUE, body, 0)

    def finish(slot):
        for k in range(TOP_K):
            pltpu.make_async_copy(ys_ref.at[pl.ds(0, tm * ch)], buf.at[slot, k], sems.at[slot]).wait()
        w = w_ref[...]
        cols = []
        for c in range(ch):
            acc = None
            for k in range(TOP_K):
                term = w[:, k:k + 1] * buf[slot, k, pl.ds(c, tm, stride=ch), :]
                acc = term if acc is None else acc + term
            cols.append(acc)
        x2 = xs1_ref[...] + g2_ref[0] * jnp.concatenate(cols, axis=1)
        if final:
            x2 = x2 * lax.rsqrt(jnp.mean(x2 * x2, axis=-1, keepdims=True) + EPS) * fg_ref[...]
        o_ref[...] = x2

    @pl.when(i == 0)
    def _():
        issue(dcur_ref, 0)

    for slot in range(2):
        @pl.when(jnp.logical_and(i + 1 < n, (i + 1) % 2 == slot))
        def _():
            issue(dnext_ref, slot)

    for slot in range(2):
        @pl.when(i % 2 == slot)
        def _():
            finish(slot)


def _combine(dest_flat, w_tok, xs1, g2, fg, ys, *, seq, final):
    N, D = xs1.shape
    ch = D // LANES
    tm = _tile(seq, 128)
    per_b = seq // tm
    n = N // tm
    return pl.pallas_call(
        functools.partial(_combine_kernel, final=final),
        out_shape=jax.ShapeDtypeStruct((N, D), F32),
        grid=(n,),
        in_specs=[pl.BlockSpec((tm * TOP_K,), lambda i: (i,), memory_space=pltpu.SMEM),
                  pl.BlockSpec((tm * TOP_K,), lambda i: (jnp.minimum(i + 1, n - 1),), memory_space=pltpu.SMEM),
                  pl.BlockSpec((tm, TOP_K), lambda i: (i, 0)),
                  pl.BlockSpec((tm, D), lambda i: (i, 0)),
                  pl.BlockSpec((1, 1, D), lambda i: (i // per_b, 0, 0)),
                  pl.BlockSpec((1, D), lambda i: (0, 0)),
                  pl.BlockSpec(memory_space=pl.ANY)],
        out_specs=pl.BlockSpec((tm, D), lambda i: (i, 0)),
        scratch_shapes=[pltpu.VMEM((2, TOP_K, tm * ch, LANES), F32), pltpu.SemaphoreType.DMA((2,))],
        compiler_params=_cparams(("arbitrary",)),
        name="moe_combine",
    )(dest_flat, dest_flat, w_tok, xs1, g2, fg, ys)


def _rope_tables(S):
    half = HEAD_DIM // 2
    inv = ROPE_THETA ** (-jnp.arange(0, half, 2, dtype=F32) / half)
    t = jnp.arange(S, dtype=jnp.int32)

    def part(pos):
        ang = pos.astype(F32)[:, None] * inv[None, :]
        c, s, z = jnp.cos(ang), jnp.sin(ang), jnp.zeros_like(ang)
        return jnp.concatenate([c, c], -1), jnp.concatenate([-s, z], -1), jnp.concatenate([z, s], -1)

    row, col, lin = part(t // GRID_W), part(t % GRID_W), part(t)
    tabs_a = tuple(jnp.tile(jnp.concatenate([r, c], -1), (1, LANES // HEAD_DIM)) for r, c in zip(row, col))
    tabs_c = tuple(jnp.pad(v, ((0, 0), (0, LANES - C_ROPE))) for v in lin)
    return tabs_a, tabs_c


def _head_slabs(w, heads, groups):
    d = w.shape[0]
    hot = (jnp.arange(heads)[:, None] // (heads // groups) == jnp.arange(groups)[None, :]).astype(w.dtype)
    return (w.reshape(d, heads, 1, HEAD_DIM) * hot[None, :, :, None]).reshape(d, heads * LANES)


def _pad_w_in(w):
    d = w.shape[0]
    a, b, c = w[:, :A_IN], w[:, A_IN:A_IN + B_IN], w[:, A_IN + B_IN:]
    kr = jnp.pad(c[:, C_Q_LORA + C_KV_LORA:], ((0, 0), (0, LANES - C_ROPE)))
    return jnp.concatenate([_head_slabs(a[:, :A_WIDTH], A_HEADS, A_KV_HEADS), a[:, A_WIDTH:],
                            _head_slabs(b[:, :B_WIDTH], B_HEADS, B_KV_HEADS), b[:, B_WIDTH:],
                            c[:, :C_Q_LORA + C_KV_LORA], kr], axis=1).astype(BF16)


def _pad_w_uq(w):
    r = w.shape[0]
    w = w.reshape(r, C_HEADS, C_NOPE + C_ROPE)
    hot = (jnp.arange(C_HEADS)[:, None] % 2 == jnp.arange(2)[None, :]).astype(w.dtype)
    nope = (w[:, :, None, :C_NOPE] * hot[None, :, :, None]).reshape(r, C_HEADS, LANES)
    rope = jnp.pad(w[:, :, C_NOPE:], ((0, 0), (0, 0), (0, LANES - C_ROPE)))
    return jnp.concatenate([nope, rope], axis=-1).reshape(r, C_HEADS * 2 * LANES).astype(BF16)


def _select_mats(heads, groups, out_w):
    h = jnp.arange(heads)[:, None, None]
    r = jnp.arange(LANES)[None, :, None]
    c = jnp.arange(out_w)[None, None, :]
    half = h // (heads // groups)
    return ((r // HEAD_DIM == half) & (c == HEAD_DIM * h + r % HEAD_DIM)).astype(BF16)


def kernel(x, c, mod_w, mod_b, norm1_g, norm2_g, w_in, a_qnorm_g, a_knorm_g, b_sink, c_qnorm_g, c_kvnorm_g, c_w_uq, c_w_uk, c_w_uv, out_norm_g, w_out, router_w, router_b, exp_w1, exp_w3, exp_w2, sh_w1, sh_w3, sh_w2, final_g):
    B, S, D = x.shape
    L = mod_w.shape[0]
    E = router_w.shape[2]
    N = B * S
    n_slots = N * TOP_K
    assert n_slots % MOE_BM == 0
    n_work = n_slots // MOE_BM + E

    mod = _mod(c, mod_w, mod_b).reshape(L, B, 6, 1, D)
    tabs_a, tabs_c = _rope_tables(S)
    esel_a = _select_mats(A_HEADS, A_KV_HEADS, A_WIDTH)
    esel_c = _select_mats(2, 2, LANES)
    slopes = (jnp.exp2(-8.0 * jnp.arange(1, B_HEADS + 1, dtype=F32) / B_HEADS) * LOG2E).reshape(B_HEADS, 1, 1)

    for l in range(L):
        sh1, sc1, g1, sh2, sc2, g2 = [mod[l, :, i] for i in range(6)]
        gq2 = jnp.tile(a_qnorm_g[l], LANES // HEAD_DIM).reshape(1, LANES)
        gk2 = jnp.tile(a_knorm_g[l], LANES // HEAD_DIM).reshape(1, LANES)
        qa, ka, va, qb, kb, vb, qc, kc, vc = _inproj(
            x, sc1, sh1, norm1_g[l].reshape(1, D), _pad_w_in(w_in[l]), tabs_a, tabs_c, gq2, gk2,
            c_qnorm_g[l].reshape(1, -1), c_kvnorm_g[l].reshape(1, -1), _pad_w_uq(c_w_uq[l]),
            c_w_uk[l].astype(BF16), c_w_uv[l].astype(BF16))
        oa = _flash(qa[:, None], ka[:, None], va[:, None], esel_a, tq=1024, tk=512, cw=512, unroll=1)
        ob = _window(qb, kb, vb, esel_a, slopes, (b_sink[l].astype(F32) * LOG2E).reshape(B_HEADS, 1, 1), tq=256)
        oc = _flash(qc.reshape(B, C_HEADS // 2, 2, S, 2 * LANES), kc, vc, esel_c, tq=2048, tk=512, cw=512, unroll=1)
        xs1, u2, scores_t = _post(
            oa, ob, oc, x, g1, sc2, sh2, g2, out_norm_g[l].reshape(1, -1), w_out[l].astype(BF16),
            norm2_g[l].reshape(1, D), router_w[l].T.astype(BF16), sh_w1[l].astype(BF16), sh_w3[l].astype(BF16),
            sh_w2[l].astype(BF16))
        idx, w_top, rank, cnt = _route(scores_t, router_b[l])
        counts = cnt[:, 0].astype(jnp.int32)
        ends = jnp.cumsum(counts)
        starts = ends - counts
        dest = _slots(idx, rank, starts).T.reshape(N * TOP_K)
        first_blk = starts // MOE_BM
        n_items = jnp.where(counts > 0, (ends - 1) // MOE_BM - first_blk + 1, 0)
        item_ends = jnp.cumsum(n_items)
        nw = item_ends[-1].reshape(1).astype(jnp.int32)
        wid = jnp.minimum(jnp.arange(n_work, dtype=jnp.int32), nw[0] - 1)
        we = jnp.minimum(jnp.searchsorted(item_ends, wid, side='right'), E - 1).astype(jnp.int32)
        wb = (first_blk[we] + wid - (item_ends - n_items)[we]).astype(jnp.int32)
        lo = jnp.clip(starts[we] - wb * MOE_BM, 0, MOE_BM).astype(jnp.int32)
        hi = jnp.clip(ends[we] - wb * MOE_BM, 0, MOE_BM).astype(jnp.int32)
        xs = _dispatch(dest, u2, n_slots, D // LANES)
        ys = _experts(wb, we, lo, hi, nw, xs, exp_w1, exp_w3, exp_w2, layer=l)
        x = _combine(dest, w_top.T, xs1.reshape(N, D), g2, final_g.reshape(1, D), ys, seq=S,
                     final=(l == L - 1)).reshape(B, S, D)
    return x
```

```python
import functools

import jax
import jax.numpy as jnp
from jax import lax
from jax.experimental import pallas as pl
from jax.experimental.pallas import tpu as pltpu

F32 = jnp.float32
BF16 = jnp.bfloat16

HEAD_DIM = 64
GRID_W = 64
ROPE_THETA = 10000.0
EPS = 1e-6
NEG_INF = -1e30
A_HEADS, A_KV_HEADS = 6, 2
B_HEADS, B_KV_HEADS = 6, 2
WINDOW = 128
C_HEADS, C_Q_LORA, C_KV_LORA, C_NOPE, C_ROPE, C_V = 4, 256, 128, 64, 32, 64
A_WIDTH = A_HEADS * HEAD_DIM
B_WIDTH = B_HEADS * HEAD_DIM
C_WIDTH = C_HEADS * C_V
A_IN = (A_HEADS + 2 * A_KV_HEADS) * HEAD_DIM
B_IN = (B_HEADS + 2 * B_KV_HEADS) * HEAD_DIM
TOP_K = 8
N_GROUPS = 8
TOPK_GROUPS = 4
ROUTED_SCALE = 2.5
LOG2E = 1.4426950408889634

LANES = 128
SUBLANE_BITS = 3
SUBLANES = 1 << SUBLANE_BITS
OFF_AQ = 0
OFF_AK = OFF_AQ + A_HEADS * LANES
OFF_AV = OFF_AK + LANES
OFF_BQ = OFF_AV + LANES
OFF_BK = OFF_BQ + B_HEADS * LANES
OFF_BV = OFF_BK + LANES
OFF_CQ = OFF_BV + LANES
OFF_CKV = OFF_CQ + C_Q_LORA
OFF_CKR = OFF_CKV + C_KV_LORA
IN_PAD = OFF_CKR + LANES

ONES_ROWS = 16
VT_ROWS = LANES + ONES_ROWS
LOOKAHEAD = 2
MOE_BM = 512
MOE_SPLIT = 2
VMEM_LIMIT = 56 * 1024 * 1024


def _tile(n, pref):
    t = min(n, pref)
    assert n % t == 0, (n, t)
    return t


def _cparams(sem, vmem=VMEM_LIMIT):
    return pltpu.CompilerParams(dimension_semantics=sem, vmem_limit_bytes=vmem)


def _mod_kernel(c_ref, w_ref, b_ref, o_ref):
    c = c_ref[...]
    ca = c * (1.0 / (1.0 + jnp.exp(-c)))
    o_ref[0] = jnp.dot(ca, w_ref[0], preferred_element_type=F32, precision=lax.Precision.HIGHEST) + b_ref[0]


def _mod(c, mod_w, mod_b):
    L, D, W = mod_w.shape
    B = c.shape[0]
    tn = _tile(W, 1536)
    return pl.pallas_call(
        _mod_kernel,
        out_shape=jax.ShapeDtypeStruct((L, B, W), F32),
        grid=(L, W // tn),
        in_specs=[
            pl.BlockSpec((B, D), lambda l, j: (0, 0)),
            pl.BlockSpec((1, D, tn), lambda l, j: (l, 0, j)),
            pl.BlockSpec((1, 1, tn), lambda l, j: (l, 0, j)),
        ],
        out_specs=pl.BlockSpec((1, B, tn), lambda l, j: (l, 0, j)),
        compiler_params=_cparams(("arbitrary", "arbitrary")),
        name="mod_proj",
    )(c, mod_w, mod_b.reshape(L, 1, W))


def _rope(v, c, sa, sb):
    return v * c + pltpu.roll(v, LANES - 16, 1) * sa + pltpu.roll(v, 16, 1) * sb


def _inproj_kernel(x_ref, sc_ref, sh_ref, g_ref, w_ref, ca_ref, saa_ref, sba_ref, cc_ref, sac_ref, sbc_ref,
                   gq_ref, gk_ref, gcq_ref, gckv_ref, wuq_ref, wuk_ref, wuv_ref,
                   qa_ref, ka_ref, va_ref, qb_ref, kb_ref, vb_ref, qc_ref, kc_ref, vc_ref, *, qs_ab, qs_c):
    x = x_ref[0]
    u = x * lax.rsqrt(jnp.mean(x * x, axis=-1, keepdims=True) + EPS) * g_ref[...]
    u = u * (1.0 + sc_ref[0]) + sh_ref[0]
    p = jnp.dot(u.astype(BF16), w_ref[...], preferred_element_type=F32)
    ca, saa, sba = ca_ref[...], saa_ref[...], sba_ref[...]
    cc, sac, sbc = cc_ref[...], sac_ref[...], sbc_ref[...]

    gq = gq_ref[...]
    for h in range(A_HEADS):
        v = p[:, OFF_AQ + LANES * h:OFF_AQ + LANES * (h + 1)]
        ss = jnp.sum(v * v, axis=-1, keepdims=True) * (1.0 / HEAD_DIM)
        v = v * lax.rsqrt(ss + EPS) * gq
        qa_ref[0, h] = (_rope(v, ca, saa, sba) * qs_ab).astype(BF16)
    k = p[:, OFF_AK:OFF_AK + LANES]
    lane = lax.broadcasted_iota(jnp.int32, k.shape, 1)
    k2 = k * k
    s0 = jnp.sum(jnp.where(lane < HEAD_DIM, k2, 0.0), axis=-1, keepdims=True)
    s1 = jnp.sum(k2, axis=-1, keepdims=True) - s0
    r = jnp.where(lane < HEAD_DIM, lax.rsqrt(s0 * (1.0 / HEAD_DIM) + EPS), lax.rsqrt(s1 * (1.0 / HEAD_DIM) + EPS))
    ka_ref[0] = _rope(k * r * gk_ref[...], ca, saa, sba).astype(BF16)
    ones = jnp.ones((ONES_ROWS, k.shape[0]), BF16)
    va_ref[0] = jnp.concatenate([p[:, OFF_AV:OFF_AV + LANES].T.astype(BF16), ones], axis=0)

    for h in range(B_HEADS):
        qb_ref[0, h] = (p[:, OFF_BQ + LANES * h:OFF_BQ + LANES * (h + 1)] * qs_ab).astype(BF16)
    kb_ref[0] = p[:, OFF_BK:OFF_BK + LANES].astype(BF16)
    vb_ref[0] = jnp.concatenate([p[:, OFF_BV:OFF_BV + LANES].T.astype(BF16), ones], axis=0)

    cq = p[:, OFF_CQ:OFF_CQ + C_Q_LORA]
    cq = cq * lax.rsqrt(jnp.mean(cq * cq, axis=-1, keepdims=True) + EPS) * gcq_ref[...]
    qh = jnp.dot(cq.astype(BF16), wuq_ref[...], preferred_element_type=F32)
    for h in range(C_HEADS):
        base = 2 * LANES * h
        qc_ref[0, h, :, 0:LANES] = (qh[:, base:base + LANES] * qs_c).astype(BF16)
        qc_ref[0, h, :, LANES:2 * LANES] = (_rope(qh[:, base + LANES:base + 2 * LANES], cc, sac, sbc) * qs_c).astype(BF16)
    ckv = p[:, OFF_CKV:OFF_CKV + C_KV_LORA]
    ckv = (ckv * lax.rsqrt(jnp.mean(ckv * ckv, axis=-1, keepdims=True) + EPS) * gckv_ref[...]).astype(BF16)
    kn = jnp.dot(ckv, wuk_ref[...], preferred_element_type=F32)
    vv = jnp.dot(ckv, wuv_ref[...], preferred_element_type=F32)
    kr = _rope(p[:, OFF_CKR:OFF_CKR + LANES], cc, sac, sbc).astype(BF16)
    for pr in range(C_HEADS // 2):
        kc_ref[0, pr, :, 0:LANES] = kn[:, LANES * pr:LANES * (pr + 1)].astype(BF16)
        kc_ref[0, pr, :, LANES:2 * LANES] = kr
        vc_ref[0, pr] = jnp.concatenate([vv[:, LANES * pr:LANES * (pr + 1)].T.astype(BF16), ones], axis=0)


def _inproj(x, sc1, sh1, g, w_pad, tabs_a, tabs_c, gq, gk, gcq, gckv, wuq_pad, wuk, wuv):
    B, S, D = x.shape
    tm = _tile(S, 512)
    full = lambda shp: pl.BlockSpec(shp, lambda b, i: (0,) * len(shp))
    tab = pl.BlockSpec((tm, LANES), lambda b, i: (i, 0))
    mod = pl.BlockSpec((1, 1, D), lambda b, i: (b, 0, 0))
    kern = functools.partial(_inproj_kernel, qs_ab=HEAD_DIM ** -0.5 * LOG2E, qs_c=(C_NOPE + C_ROPE) ** -0.5 * LOG2E)
    outs = (
        jax.ShapeDtypeStruct((B, A_HEADS, S, LANES), BF16), jax.ShapeDtypeStruct((B, S, LANES), BF16),
        jax.ShapeDtypeStruct((B, VT_ROWS, S), BF16),
        jax.ShapeDtypeStruct((B, B_HEADS, S, LANES), BF16), jax.ShapeDtypeStruct((B, S, LANES), BF16),
        jax.ShapeDtypeStruct((B, VT_ROWS, S), BF16),
        jax.ShapeDtypeStruct((B, C_HEADS, S, 2 * LANES), BF16), jax.ShapeDtypeStruct((B, C_HEADS // 2, S, 2 * LANES), BF16),
        jax.ShapeDtypeStruct((B, C_HEADS // 2, VT_ROWS, S), BF16),
    )
    slab = pl.BlockSpec((1, tm, LANES), lambda b, i: (b, i, 0))
    out_specs = (
        pl.BlockSpec((1, A_HEADS, tm, LANES), lambda b, i: (b, 0, i, 0)), slab,
        pl.BlockSpec((1, VT_ROWS, tm), lambda b, i: (b, 0, i)),
        pl.BlockSpec((1, B_HEADS, tm, LANES), lambda b, i: (b, 0, i, 0)), slab,
        pl.BlockSpec((1, VT_ROWS, tm), lambda b, i: (b, 0, i)),
        pl.BlockSpec((1, C_HEADS, tm, 2 * LANES), lambda b, i: (b, 0, i, 0)),
        pl.BlockSpec((1, C_HEADS // 2, tm, 2 * LANES), lambda b, i: (b, 0, i, 0)),
        pl.BlockSpec((1, C_HEADS // 2, VT_ROWS, tm), lambda b, i: (b, 0, 0, i)),
    )
    return pl.pallas_call(
        kern, out_shape=outs, grid=(B, S // tm),
        in_specs=[pl.BlockSpec((1, tm, D), lambda b, i: (b, i, 0)), mod, mod, full((1, D)), full((D, IN_PAD)),
                  tab, tab, tab, tab, tab, tab,
                  full((1, LANES)), full((1, LANES)), full((1, C_Q_LORA)), full((1, C_KV_LORA)),
                  full(wuq_pad.shape), full(wuk.shape), full(wuv.shape)],
        out_specs=out_specs,
        compiler_params=_cparams(("arbitrary", "arbitrary")),
        name="norm_inproj",
    )(x, sc1, sh1, g, w_pad, *tabs_a, *tabs_c, gq, gk, gcq, gckv, wuq_pad, wuk, wuv)


def _flash_kernel(q_ref, k_ref, vt_ref, e_ref, o_ref, qt_ref, m_ref, acc_ref, *, tk, cw, unroll):
    H, tq, dk = q_ref.shape[2:]
    S = k_ref.shape[2]
    R = H * tq
    for h in range(H):
        qt_ref[:, h * tq:(h + 1) * tq] = q_ref[0, 0, h].astype(F32).T.astype(BF16)
    m_ref[...] = jnp.full(m_ref.shape, NEG_INF, F32)
    acc_ref[...] = jnp.zeros(acc_ref.shape, F32)

    def body(j, carry):
        off = pl.multiple_of(j * tk, tk)
        k = k_ref[0, 0, pl.ds(off, tk), :]
        vt = vt_ref[0, 0, :, pl.ds(off, tk)]
        nc = R // cw
        scores = lambda c: jnp.dot(k, qt_ref[:, c * cw:(c + 1) * cw], preferred_element_type=F32)
        accs, ms = [], []
        pending = [scores(c) for c in range(min(LOOKAHEAD, nc))]
        for c in range(nc):
            sl = slice(c * cw, (c + 1) * cw)
            s = pending.pop(0)
            if c + LOOKAHEAD < nc:
                pending.append(scores(c + LOOKAHEAD))
            m_prev = m_ref[:, sl]
            m_new = jnp.maximum(m_prev, jnp.max(s, axis=0, keepdims=True))
            p = jnp.exp2(s - m_new).astype(BF16)
            alpha = jnp.exp2(m_prev - m_new)
            accs.append(alpha * acc_ref[:, sl] + jnp.dot(vt, p, preferred_element_type=F32))
            ms.append(m_new)
        acc_ref[...] = jnp.concatenate(accs, axis=1)
        m_ref[...] = jnp.concatenate(ms, axis=1)
        return carry

    lax.fori_loop(0, S // tk, body, 0, unroll=unroll)
    o_t = acc_ref[0:LANES, :] / acc_ref[LANES:LANES + 1, :]
    out = None
    for h in range(H):
        o_h = o_t[:, h * tq:(h + 1) * tq].T.astype(BF16)
        term = jnp.dot(o_h, e_ref[h], preferred_element_type=F32)
        out = term if out is None else out + term
    o_ref[0] = out.astype(BF16)


def _flash(q, k, vt, esel, *, tq, tk, cw=256, unroll=2):
    B, P, H, S, dk = q.shape
    W = esel.shape[2]
    tq = _tile(S, tq)
    tk = _tile(S, tk)
    R = H * tq
    cw = _tile(R, cw)
    return pl.pallas_call(
        functools.partial(_flash_kernel, tk=tk, cw=cw, unroll=unroll),
        out_shape=jax.ShapeDtypeStruct((B, S, P * W), BF16),
        grid=(B, P, S // tq),
        in_specs=[pl.BlockSpec((1, 1, H, tq, dk), lambda b, p, i: (b, p, 0, i, 0)),
                  pl.BlockSpec((1, 1, S, dk), lambda b, p, i: (b, p, 0, 0)),
                  pl.BlockSpec((1, 1, VT_ROWS, S), lambda b, p, i: (b, p, 0, 0)),
                  pl.BlockSpec((H, LANES, W), lambda b, p, i: (0, 0, 0))],
        out_specs=pl.BlockSpec((1, tq, W), lambda b, p, i: (b, i, p)),
        scratch_shapes=[pltpu.VMEM((dk, R), BF16), pltpu.VMEM((1, R), F32), pltpu.VMEM((VT_ROWS, R), F32)],
        compiler_params=_cparams(("arbitrary", "arbitrary", "arbitrary")),
        name="flash_attn",
    )(q, k, vt, esel)


def _window_kernel(q_ref, k_ref, vt_ref, e_ref, slope_ref, sink_ref, o_ref, *, tq):
    H = q_ref.shape[1]
    S = k_ref.shape[1]
    wk = min(S, tq + 2 * WINDOW)
    i = pl.program_id(1)
    start = pl.multiple_of(jnp.clip(i * tq - WINDOW, 0, S - wk), WINDOW)
    k = k_ref[0, pl.ds(start, wk), :]
    vt = vt_ref[0, :, pl.ds(start, wk)]
    spos = start + lax.broadcasted_iota(jnp.int32, (wk, tq), 0)
    tpos = i * tq + lax.broadcasted_iota(jnp.int32, (wk, tq), 1)
    dist = jnp.abs(spos - tpos)
    inside = dist <= WINDOW
    distf = dist.astype(F32)
    qt = jnp.concatenate([q_ref[0, h].astype(F32).T.astype(BF16) for h in range(H)], axis=1)
    s_all = jnp.dot(k, qt, preferred_element_type=F32)
    es, ms = [], []
    for h in range(H):
        s = s_all[:, h * tq:(h + 1) * tq] - slope_ref[h] * distf
        s = jnp.where(inside, s, NEG_INF)
        m = jnp.maximum(jnp.max(s, axis=0, keepdims=True), sink_ref[h])
        es.append(jnp.exp2(s - m).astype(BF16))
        ms.append(m)
    acc = jnp.dot(vt, jnp.concatenate(es, axis=1), preferred_element_type=F32)
    out = None
    for h in range(H):
        sl = slice(h * tq, (h + 1) * tq)
        den = acc[LANES:LANES + 1, sl] + jnp.exp2(sink_ref[h] - ms[h])
        o = (acc[0:LANES, sl] / den).T.astype(BF16)
        term = jnp.dot(o, e_ref[h], preferred_element_type=F32)
        out = term if out is None else out + term
    o_ref[0] = out.astype(BF16)


def _window(q, k, v, esel, slopes, sink, *, tq):
    B, H, S, _ = q.shape
    W = esel.shape[2]
    tq = _tile(S, tq)
    return pl.pallas_call(
        functools.partial(_window_kernel, tq=tq),
        out_shape=jax.ShapeDtypeStruct((B, S, W), BF16),
        grid=(B, S // tq),
        in_specs=[pl.BlockSpec((1, H, tq, LANES), lambda b, i: (b, 0, i, 0)),
                  pl.BlockSpec((1, S, LANES), lambda b, i: (b, 0, 0)),
                  pl.BlockSpec((1, VT_ROWS, S), lambda b, i: (b, 0, 0)),
                  pl.BlockSpec((H, LANES, W), lambda b, i: (0, 0, 0)),
                  pl.BlockSpec((H, 1, 1), lambda b, i: (0, 0, 0)),
                  pl.BlockSpec((H, 1, 1), lambda b, i: (0, 0, 0))],
        out_specs=pl.BlockSpec((1, tq, W), lambda b, i: (b, i, 0)),
        compiler_params=_cparams(("arbitrary", "arbitrary")),
        name="window_attn",
    )(q, k, v, esel, slopes, sink)


def _sigmoid(v):
    return 1.0 / (1.0 + jnp.exp(-v))


def _store_token_tiles(ref, x):
    ch = x.shape[1] // LANES
    for c in range(ch):
        ref[pl.ds(c, x.shape[0], stride=ch), :] = x[:, LANES * c:LANES * (c + 1)]


def _load_token_tiles(ref, n, ch):
    return [ref[pl.ds(c, n, stride=ch), :] for c in range(ch)]


def _post_kernel(oa_ref, ob_ref, oc_ref, x_ref, g1_ref, sc2_ref, sh2_ref, g2_ref, og_ref, wout_ref, n2_ref, rwt_ref,
                 sw1_ref, sw3_ref, sw2_ref, xs1_ref, u2_ref, st_ref):
    og = og_ref[...]

    def rn(o_ref, g):
        o = o_ref[0].astype(F32)
        return (o * lax.rsqrt(jnp.mean(o * o, axis=-1, keepdims=True) + EPS) * g).astype(BF16)

    mixed = jnp.concatenate([rn(oa_ref, og[:, :A_WIDTH]), rn(ob_ref, og[:, A_WIDTH:A_WIDTH + B_WIDTH]),
                             rn(oc_ref, og[:, A_WIDTH + B_WIDTH:])], axis=-1)
    x1 = x_ref[0] + g1_ref[0] * jnp.dot(mixed, wout_ref[...], preferred_element_type=F32)
    u2 = x1 * lax.rsqrt(jnp.mean(x1 * x1, axis=-1, keepdims=True) + EPS) * n2_ref[...]
    u2 = u2 * (1.0 + sc2_ref[0]) + sh2_ref[0]
    _store_token_tiles(u2_ref, u2)
    ub = u2.astype(BF16)
    logits_t = lax.dot_general(rwt_ref[...], ub, (((1,), (1,)), ((), ())), preferred_element_type=F32)
    st_ref[...] = _sigmoid(logits_t)
    h1 = jnp.dot(ub, sw1_ref[...], preferred_element_type=F32)
    h3 = jnp.dot(ub, sw3_ref[...], preferred_element_type=F32)
    hid = (h1 * _sigmoid(h1) * h3).astype(BF16)
    xs1_ref[0] = x1 + g2_ref[0] * jnp.dot(hid, sw2_ref[...], preferred_element_type=F32)


def _post(oa, ob, oc, x, g1, sc2, sh2, g2, og, wout, n2, rwt, sw1, sw3, sw2):
    B, S, D = x.shape
    E = rwt.shape[0]
    tm = _tile(S, 512)
    nt = S // tm
    full = lambda a: pl.BlockSpec(a.shape, lambda b, i: (0,) * a.ndim)
    mod = pl.BlockSpec((1, 1, D), lambda b, i: (b, 0, 0))
    row = lambda w: pl.BlockSpec((1, tm, w), lambda b, i: (b, i, 0))
    return pl.pallas_call(
        _post_kernel,
        out_shape=(jax.ShapeDtypeStruct((B, S, D), F32), jax.ShapeDtypeStruct((B * S * D // LANES, LANES), F32),
                   jax.ShapeDtypeStruct((E, B * S), F32)),
        grid=(B, nt),
        in_specs=[row(A_WIDTH), row(B_WIDTH), row(C_WIDTH), row(D), mod, mod, mod, mod, full(og), full(wout), full(n2),
                  full(rwt), full(sw1), full(sw3), full(sw2)],
        out_specs=(row(D), pl.BlockSpec((tm * D // LANES, LANES), lambda b, i: (b * nt + i, 0)),
                   pl.BlockSpec((E, tm), lambda b, i: (0, b * nt + i))),
        compiler_params=_cparams(("arbitrary", "arbitrary")),
        name="post_attn",
    )(oa, ob, oc, x, g1, sc2, sh2, g2, og, wout, n2, rwt, sw1, sw3, sw2)


def _route_kernel(st_ref, rb_ref, tri_ref, idx_ref, w_ref, rank_ref, cnt_ref, carry_ref):
    E, tm = st_ref.shape
    per = E // N_GROUPS
    big = float(E)

    @pl.when(pl.program_id(0) == 0)
    def _():
        carry_ref[...] = jnp.zeros(carry_ref.shape, F32)

    s = st_ref[...]
    choice = s + rb_ref[...]
    c3 = choice.reshape(N_GROUPS, per, tm)
    io3 = lax.broadcasted_iota(jnp.int32, c3.shape, 1).astype(F32)
    m1 = jnp.max(c3, axis=1, keepdims=True)
    i1 = jnp.min(jnp.where(c3 == m1, io3, big), axis=1, keepdims=True)
    m2 = jnp.max(jnp.where(io3 == i1, -jnp.inf, c3), axis=1, keepdims=True)
    gs = (m1 + m2).reshape(N_GROUPS, tm)
    gio = lax.broadcasted_iota(jnp.int32, gs.shape, 0).astype(F32)
    keep = jnp.zeros(gs.shape, F32)
    for _ in range(TOPK_GROUPS):
        mg = jnp.max(gs, axis=0, keepdims=True)
        ig = jnp.min(jnp.where(gs == mg, gio, big), axis=0, keepdims=True)
        sel = gio == ig
        keep = jnp.where(sel, 1.0, keep)
        gs = jnp.where(sel, -jnp.inf, gs)
    cur = jnp.where(keep.reshape(N_GROUPS, 1, tm) > 0.5, c3, NEG_INF).reshape(E, tm)
    eio = lax.broadcasted_iota(jnp.int32, (E, tm), 0).astype(F32)
    sels, idxs, ws = [], [], []
    for _ in range(TOP_K):
        m = jnp.max(cur, axis=0, keepdims=True)
        ik = jnp.min(jnp.where(cur == m, eio, big), axis=0, keepdims=True)
        sel = eio == ik
        ws.append(jnp.sum(jnp.where(sel, s, 0.0), axis=0, keepdims=True))
        cur = jnp.where(sel, -jnp.inf, cur)
        sels.append(sel)
        idxs.append(ik)
    wsum = ws[0]
    for wk in ws[1:]:
        wsum = wsum + wk
    w_ref[...] = jnp.concatenate([wk / wsum * ROUTED_SCALE for wk in ws], axis=0)
    idx_ref[...] = jnp.concatenate(idxs, axis=0).astype(jnp.int32)
    onehot = jnp.zeros((E, tm), F32)
    for sel in sels:
        onehot = jnp.where(sel, 1.0, onehot)
    prefix = jnp.dot(onehot.astype(BF16), tri_ref[...], preferred_element_type=F32) + carry_ref[...]
    rank_ref[...] = jnp.concatenate(
        [jnp.sum(jnp.where(sel, prefix, 0.0), axis=0, keepdims=True) for sel in sels], axis=0).astype(jnp.int32)
    carry = carry_ref[...] + jnp.sum(onehot, axis=1, keepdims=True)
    carry_ref[...] = carry
    cnt_ref[...] = jnp.broadcast_to(carry, cnt_ref.shape)


def _route(scores_t, router_b):
    E, N = scores_t.shape
    tm = _tile(N, 512)
    tri = (jnp.arange(tm)[:, None] < jnp.arange(tm)[None, :]).astype(BF16)
    return pl.pallas_call(
        _route_kernel,
        out_shape=(jax.ShapeDtypeStruct((TOP_K, N), jnp.int32), jax.ShapeDtypeStruct((TOP_K, N), F32),
                   jax.ShapeDtypeStruct((TOP_K, N), jnp.int32), jax.ShapeDtypeStruct((E, LANES), F32)),
        grid=(N // tm,),
        in_specs=[pl.BlockSpec((E, tm), lambda i: (0, i)), pl.BlockSpec((E, 1), lambda i: (0, 0)),
                  pl.BlockSpec((tm, tm), lambda i: (0, 0))],
        out_specs=(pl.BlockSpec((TOP_K, tm), lambda i: (0, i)), pl.BlockSpec((TOP_K, tm), lambda i: (0, i)),
                   pl.BlockSpec((TOP_K, tm), lambda i: (0, i)), pl.BlockSpec((E, LANES), lambda i: (0, 0))),
        scratch_shapes=[pltpu.VMEM((E, 1), F32)],
        compiler_params=_cparams(("arbitrary",)),
        name="route_topk",
    )(scores_t, router_b.reshape(E, 1), tri)


def _slots_kernel(idx_ref, rank_ref, starts_ref, dest_ref):
    E = starts_ref.shape[0]
    tm = idx_ref.shape[1]
    eio = lax.broadcasted_iota(jnp.int32, (E, tm), 0)
    starts = starts_ref[...]
    rows = [jnp.sum(jnp.where(eio == idx_ref[k:k + 1, :], starts, 0.0), axis=0, keepdims=True) for k in range(TOP_K)]
    dest_ref[...] = jnp.concatenate(rows, axis=0).astype(jnp.int32) + rank_ref[...]


def _slots(idx, rank, starts):
    K, N = idx.shape
    E = starts.shape[0]
    tm = _tile(N, 512)
    blk = pl.BlockSpec((K, tm), lambda i: (0, i))
    return pl.pallas_call(
        _slots_kernel, out_shape=jax.ShapeDtypeStruct((K, N), jnp.int32), grid=(N // tm,),
        in_specs=[blk, blk, pl.BlockSpec((E, 1), lambda i: (0, 0))], out_specs=blk,
        compiler_params=_cparams(("arbitrary",)), name="moe_slots",
    )(idx, rank, starts.astype(F32).reshape(E, 1))


TOKENS_PER_ISSUE = 8


def _dispatch_kernel(dest_ref, u_ref, xs_ref, sem, *, ch):
    tm = u_ref.shape[0] // ch

    def body(g, carry):
        base = pl.multiple_of(g * (TOKENS_PER_ISSUE * ch), TOKENS_PER_ISSUE * ch)
        for r in range(TOKENS_PER_ISSUE):
            for k in range(TOP_K):
                d = dest_ref[g * (TOKENS_PER_ISSUE * TOP_K) + r * TOP_K + k]
                pltpu.make_async_copy(u_ref.at[pl.ds(base + r * ch, ch)],
                                      xs_ref.at[pl.ds(pl.multiple_of(d * ch, ch), ch)],
                                      sem).start(priority=k % 2)
        return carry

    lax.fori_loop(0, tm // TOKENS_PER_ISSUE, body, 0)
    for k in range(TOP_K):
        pltpu.make_async_copy(u_ref, xs_ref.at[pl.ds(0, tm * ch)], sem).wait()


def _dispatch(dest_flat, u2t, n_slots, ch):
    N = u2t.shape[0] // ch
    tm = _tile(N, 256)
    return pl.pallas_call(
        functools.partial(_dispatch_kernel, ch=ch),
        out_shape=jax.ShapeDtypeStruct((n_slots * ch, LANES), F32),
        grid=(N // tm,),
        in_specs=[pl.BlockSpec((tm * TOP_K,), lambda i: (i,), memory_space=pltpu.SMEM),
                  pl.BlockSpec((tm * ch, LANES), lambda i: (i, 0))],
        out_specs=pl.BlockSpec(memory_space=pl.ANY),
        scratch_shapes=[pltpu.SemaphoreType.DMA(())],
        compiler_params=pltpu.CompilerParams(dimension_semantics=("arbitrary",), vmem_limit_bytes=VMEM_LIMIT,
                                             has_side_effects=True),
        name="moe_dispatch",
    )(dest_flat, u2t)


def _expert_kernel(wb_ref, we_ref, lo_ref, hi_ref, nw_ref, xs_ref, w1_ref, w3_ref, w2_ref, ys_ref, w1b, w3b, w2b):
    j = pl.program_id(0)

    @pl.when(j < nw_ref[0])
    def _():
        prev = jnp.maximum(j - 1, 0)
        new_expert = jnp.logical_or(j == 0, we_ref[j] != we_ref[prev])
        new_block = jnp.logical_or(j == 0, wb_ref[j] != wb_ref[prev])

        @pl.when(new_expert)
        def _():
            w1b[...] = w1_ref[0].astype(BF16)
            w3b[...] = w3_ref[0].astype(BF16)
            w2b[...] = w2_ref[0].astype(BF16)

        ch = w1b.shape[0] // LANES
        rows = lax.broadcasted_iota(jnp.int32, (MOE_BM, LANES), 0)
        mine = (rows >= lo_ref[j]) & (rows < hi_ref[j])
        ys = []
        for s in range(MOE_SPLIT):
            r0, nr = s * (MOE_BM // MOE_SPLIT), MOE_BM // MOE_SPLIT
            xb = jnp.concatenate(
                [jnp.where(mine[r0:r0 + nr], xs_ref[pl.ds(r0 * ch + c, nr, stride=ch), :], 0.0).astype(BF16)
                 for c in range(ch)], axis=1)
            h1 = jnp.dot(xb, w1b[...], preferred_element_type=F32)
            h3 = jnp.dot(xb, w3b[...], preferred_element_type=F32)
            hid = (h1 * _sigmoid(h1) * h3).astype(BF16)
            ys.append(jnp.dot(hid, w2b[...], preferred_element_type=F32))
        y = jnp.concatenate(ys, axis=0)

        @pl.when(new_block)
        def _():
            _store_token_tiles(ys_ref, y)

        @pl.when(jnp.logical_not(new_block))
        def _():
            for c, prev in enumerate(_load_token_tiles(ys_ref, MOE_BM, ch)):
                ys_ref[pl.ds(c, MOE_BM, stride=ch), :] = jnp.where(mine, y[:, LANES * c:LANES * (c + 1)], prev)


def _experts(wb, we, lo, hi, nw, xs, w1, w3, w2, *, layer):
    D, F = w1.shape[2:]
    ch = D // LANES
    rows = MOE_BM * ch
    blk = lambda j, wb, we, lo, hi, nw: (wb[j], 0)
    wsel = lambda j, wb, we, lo, hi, nw: (layer, we[j], 0, 0)
    grid_spec = pltpu.PrefetchScalarGridSpec(
        num_scalar_prefetch=5, grid=(wb.shape[0],),
        in_specs=[pl.BlockSpec((rows, LANES), blk), pl.BlockSpec((None, 1, D, F), wsel),
                  pl.BlockSpec((None, 1, D, F), wsel), pl.BlockSpec((None, 1, F, D), wsel)],
        out_specs=pl.BlockSpec((rows, LANES), blk),
        scratch_shapes=[pltpu.VMEM((D, F), BF16), pltpu.VMEM((D, F), BF16), pltpu.VMEM((F, D), BF16)])
    return pl.pallas_call(
        _expert_kernel, out_shape=jax.ShapeDtypeStruct(xs.shape, F32), grid_spec=grid_spec,
        compiler_params=_cparams(("arbitrary",)), name="moe_experts",
    )(wb, we, lo, hi, nw, xs, w1, w3, w2)


def _combine_kernel(dcur_ref, dnext_ref, w_ref, xs1_ref, g2_ref, fg_ref, ys_ref, o_ref, buf, sems, *, final):
    tm, D = xs1_ref.shape
    ch = D // LANES
    i = pl.program_id(0)
    n = pl.num_programs(0)

    def issue(dref, slot):
        def body(g, carry):
            base = pl.multiple_of(g * (TOKENS_PER_ISSUE * ch), TOKENS_PER_ISSUE * ch)
            for r in range(TOKENS_PER_ISSUE):
                for k in range(TOP_K):
                    d = dref[g * (TOKENS_PER_ISSUE * TOP_K) + r * TOP_K + k]
                    pltpu.make_async_copy(ys_ref.at[pl.ds(pl.multiple_of(d * ch, ch), ch)],
                                          buf.at[slot, k, pl.ds(base + r * ch, ch)],
                                          sems.at[slot]).start(priority=k % 2)
            return carry

        lax.fori_loop(0, tm // TOKENS_PER_ISSUE, body, 0)

    def finish(slot):
        for k in range(TOP_K):
            pltpu.make_async_copy(ys_ref.at[pl.ds(0, tm * ch)], buf.at[slot, k], sems.at[slot]).wait()
        w = w_ref[...]
        cols = []
        for c in range(ch):
            acc = None
            for k in range(TOP_K):
                term = w[:, k:k + 1] * buf[slot, k, pl.ds(c, tm, stride=ch), :]
                acc = term if acc is None else acc + term
            cols.append(acc)
        x2 = xs1_ref[...] + g2_ref[0] * jnp.concatenate(cols, axis=1)
        if final:
            x2 = x2 * lax.rsqrt(jnp.mean(x2 * x2, axis=-1, keepdims=True) + EPS) * fg_ref[...]
        o_ref[...] = x2

    @pl.when(i == 0)
    def _():
        issue(dcur_ref, 0)

    for slot in range(2):
        @pl.when(jnp.logical_and(i + 1 < n, (i + 1) % 2 == slot))
        def _():
            issue(dnext_ref, slot)

    for slot in range(2):
        @pl.when(i % 2 == slot)
        def _():
            finish(slot)


def _combine(dest_flat, w_tok, xs1, g2, fg, ys, *, seq, final):
    N, D = xs1.shape
    ch = D // LANES
    tm = _tile(seq, 128)
    per_b = seq // tm
    n = N // tm
    return pl.pallas_call(
        functools.partial(_combine_kernel, final=final),
        out_shape=jax.ShapeDtypeStruct((N, D), F32),
        grid=(n,),
        in_specs=[pl.BlockSpec((tm * TOP_K,), lambda i: (i,), memory_space=pltpu.SMEM),
                  pl.BlockSpec((tm * TOP_K,), lambda i: (jnp.minimum(i + 1, n - 1),), memory_space=pltpu.SMEM),
                  pl.BlockSpec((tm, TOP_K), lambda i: (i, 0)),
                  pl.BlockSpec((tm, D), lambda i: (i, 0)),
                  pl.BlockSpec((1, 1, D), lambda i: (i // per_b, 0, 0)),
                  pl.BlockSpec((1, D), lambda i: (0, 0)),
                  pl.BlockSpec(memory_space=pl.ANY)],
        out_specs=pl.BlockSpec((tm, D), lambda i: (i, 0)),
        scratch_shapes=[pltpu.VMEM((2, TOP_K, tm * ch, LANES), F32), pltpu.SemaphoreType.DMA((2,))],
        compiler_params=_cparams(("arbitrary",)),
        name="moe_combine",
    )(dest_flat, dest_flat, w_tok, xs1, g2, fg, ys)


def _rope_tables(S):
    half = HEAD_DIM // 2
    inv = ROPE_THETA ** (-jnp.arange(0, half, 2, dtype=F32) / half)
    t = jnp.arange(S, dtype=jnp.int32)

    def part(pos):
        ang = pos.astype(F32)[:, None] * inv[None, :]
        c, s, z = jnp.cos(ang), jnp.sin(ang), jnp.zeros_like(ang)
        return jnp.concatenate([c, c], -1), jnp.concatenate([-s, z], -1), jnp.concatenate([z, s], -1)

    row, col, lin = part(t // GRID_W), part(t % GRID_W), part(t)
    tabs_a = tuple(jnp.tile(jnp.concatenate([r, c], -1), (1, LANES // HEAD_DIM)) for r, c in zip(row, col))
    tabs_c = tuple(jnp.pad(v, ((0, 0), (0, LANES - C_ROPE))) for v in lin)
    return tabs_a, tabs_c


def _head_slabs(w, heads, groups):
    d = w.shape[0]
    hot = (jnp.arange(heads)[:, None] // (heads // groups) == jnp.arange(groups)[None, :]).astype(w.dtype)
    return (w.reshape(d, heads, 1, HEAD_DIM) * hot[None, :, :, None]).reshape(d, heads * LANES)


def _pad_w_in(w):
    d = w.shape[0]
    a, b, c = w[:, :A_IN], w[:, A_IN:A_IN + B_IN], w[:, A_IN + B_IN:]
    kr = jnp.pad(c[:, C_Q_LORA + C_KV_LORA:], ((0, 0), (0, LANES - C_ROPE)))
    return jnp.concatenate([_head_slabs(a[:, :A_WIDTH], A_HEADS, A_KV_HEADS), a[:, A_WIDTH:],
                            _head_slabs(b[:, :B_WIDTH], B_HEADS, B_KV_HEADS), b[:, B_WIDTH:],
                            c[:, :C_Q_LORA + C_KV_LORA], kr], axis=1).astype(BF16)


def _pad_w_uq(w):
    r = w.shape[0]
    w = w.reshape(r, C_HEADS, C_NOPE + C_ROPE)
    hot = (jnp.arange(C_HEADS)[:, None] % 2 == jnp.arange(2)[None, :]).astype(w.dtype)
    nope = (w[:, :, None, :C_NOPE] * hot[None, :, :, None]).reshape(r, C_HEADS, LANES)
    rope = jnp.pad(w[:, :, C_NOPE:], ((0, 0), (0, 0), (0, LANES - C_ROPE)))
    return jnp.concatenate([nope, rope], axis=-1).reshape(r, C_HEADS * 2 * LANES).astype(BF16)


def _select_mats(heads, groups, out_w):
    h = jnp.arange(heads)[:, None, None]
    r = jnp.arange(LANES)[None, :, None]
    c = jnp.arange(out_w)[None, None, :]
    half = h // (heads // groups)
    return ((r // HEAD_DIM == half) & (c == HEAD_DIM * h + r % HEAD_DIM)).astype(BF16)


def kernel(x, c, mod_w, mod_b, norm1_g, norm2_g, w_in, a_qnorm_g, a_knorm_g, b_sink, c_qnorm_g, c_kvnorm_g, c_w_uq, c_w_uk, c_w_uv, out_norm_g, w_out, router_w, router_b, exp_w1, exp_w3, exp_w2, sh_w1, sh_w3, sh_w2, final_g):
    B, S, D = x.shape
    L = mod_w.shape[0]
    E = router_w.shape[2]
    N = B * S
    n_slots = N * TOP_K
    assert n_slots % MOE_BM == 0
    n_work = n_slots // MOE_BM + E

    mod = _mod(c, mod_w, mod_b).reshape(L, B, 6, 1, D)
    tabs_a, tabs_c = _rope_tables(S)
    esel_a = _select_mats(A_HEADS, A_KV_HEADS, A_WIDTH)
    esel_c = _select_mats(2, 2, LANES)
    slopes = (jnp.exp2(-8.0 * jnp.arange(1, B_HEADS + 1, dtype=F32) / B_HEADS) * LOG2E).reshape(B_HEADS, 1, 1)

    for l in range(L):
        sh1, sc1, g1, sh2, sc2, g2 = [mod[l, :, i] for i in range(6)]
        gq2 = jnp.tile(a_qnorm_g[l], LANES // HEAD_DIM).reshape(1, LANES)
        gk2 = jnp.tile(a_knorm_g[l], LANES // HEAD_DIM).reshape(1, LANES)
        qa, ka, va, qb, kb, vb, qc, kc, vc = _inproj(
            x, sc1, sh1, norm1_g[l].reshape(1, D), _pad_w_in(w_in[l]), tabs_a, tabs_c, gq2, gk2,
            c_qnorm_g[l].reshape(1, -1), c_kvnorm_g[l].reshape(1, -1), _pad_w_uq(c_w_uq[l]),
            c_w_uk[l].astype(BF16), c_w_uv[l].astype(BF16))
        oa = _flash(qa[:, None], ka[:, None], va[:, None], esel_a, tq=1024, tk=512, cw=512, unroll=1)
        ob = _window(qb, kb, vb, esel_a, slopes, (b_sink[l].astype(F32) * LOG2E).reshape(B_HEADS, 1, 1), tq=256)
        oc = _flash(qc.reshape(B, C_HEADS // 2, 2, S, 2 * LANES), kc, vc, esel_c, tq=2048, tk=512, cw=512, unroll=1)
        xs1, u2, scores_t = _post(
            oa, ob, oc, x, g1, sc2, sh2, g2, out_norm_g[l].reshape(1, -1), w_out[l].astype(BF16),
            norm2_g[l].reshape(1, D), router_w[l].T.astype(BF16), sh_w1[l].astype(BF16), sh_w3[l].astype(BF16),
            sh_w2[l].astype(BF16))
        idx, w_top, rank, cnt = _route(scores_t, router_b[l])
        counts = cnt[:, 0].astype(jnp.int32)
        ends = jnp.cumsum(counts)
        starts = ends - counts
        dest = _slots(idx, rank, starts).T.reshape(N * TOP_K)
        first_blk = starts // MOE_BM
        n_items = jnp.where(counts > 0, (ends - 1) // MOE_BM - first_blk + 1, 0)
        item_ends = jnp.cumsum(n_items)
        nw = item_ends[-1].reshape(1).astype(jnp.int32)
        wid = jnp.minimum(jnp.arange(n_work, dtype=jnp.int32), nw[0] - 1)
        we = jnp.minimum(jnp.sum(item_ends[None, :] <= wid[:, None], axis=1), E - 1).astype(jnp.int32)
        hot = we[:, None] == jnp.arange(E, dtype=jnp.int32)[None, :]
        take = lambda tbl: jnp.sum(jnp.where(hot, tbl[None, :], 0), axis=1).astype(jnp.int32)
        wb = take(first_blk) + wid - take(item_ends - n_items)
        lo = jnp.clip(take(starts) - wb * MOE_BM, 0, MOE_BM).astype(jnp.int32)
        hi = jnp.clip(take(ends) - wb * MOE_BM, 0, MOE_BM).astype(jnp.int32)
        xs = _dispatch(dest, u2, n_slots, D // LANES)
        ys = _experts(wb, we, lo, hi, nw, xs, exp_w1, exp_w3, exp_w2, layer=l)
        x = _combine(dest, w_top.T, xs1.reshape(N, D), g2, final_g.reshape(1, D), ys, seq=S,
                     final=(l == L - 1)).reshape(B, S, D)
    return x
```

```python
import functools

import jax
import jax.numpy as jnp
from jax import lax
from jax.experimental import pallas as pl
from jax.experimental.pallas import tpu as pltpu

F32 = jnp.float32
BF16 = jnp.bfloat16

HEAD_DIM = 64
GRID_W = 64
ROPE_THETA = 10000.0
EPS = 1e-6
NEG_INF = -1e30
A_HEADS, A_KV_HEADS = 6, 2
B_HEADS, B_KV_HEADS = 6, 2
WINDOW = 128
C_HEADS, C_Q_LORA, C_KV_LORA, C_NOPE, C_ROPE, C_V = 4, 256, 128, 64, 32, 64
A_WIDTH = A_HEADS * HEAD_DIM
B_WIDTH = B_HEADS * HEAD_DIM
C_WIDTH = C_HEADS * C_V
A_IN = (A_HEADS + 2 * A_KV_HEADS) * HEAD_DIM
B_IN = (B_HEADS + 2 * B_KV_HEADS) * HEAD_DIM
TOP_K = 8
N_GROUPS = 8
TOPK_GROUPS = 4
ROUTED_SCALE = 2.5
LOG2E = 1.4426950408889634

LANES = 128
SUBLANE_BITS = 3
SUBLANES = 1 << SUBLANE_BITS
OFF_AQ = 0
OFF_AK = OFF_AQ + A_HEADS * LANES
OFF_AV = OFF_AK + LANES
OFF_BQ = OFF_AV + LANES
OFF_BK = OFF_BQ + B_HEADS * LANES
OFF_BV = OFF_BK + LANES
OFF_CQ = OFF_BV + LANES
OFF_CKV = OFF_CQ + C_Q_LORA
OFF_CKR = OFF_CKV + C_KV_LORA
IN_PAD = OFF_CKR + LANES

ONES_ROWS = 16
VT_ROWS = LANES + ONES_ROWS
LOOKAHEAD = 2
MOE_BM = 512
MOE_SPLIT = 2
VMEM_LIMIT = 56 * 1024 * 1024


def _tile(n, pref):
    t = min(n, pref)
    assert n % t == 0, (n, t)
    return t


def _cparams(sem, vmem=VMEM_LIMIT):
    return pltpu.CompilerParams(dimension_semantics=sem, vmem_limit_bytes=vmem)


def _mod_kernel(c_ref, w_ref, b_ref, o_ref):
    c = c_ref[...]
    ca = c * (1.0 / (1.0 + jnp.exp(-c)))
    o_ref[0] = jnp.dot(ca, w_ref[0], preferred_element_type=F32, precision=lax.Precision.HIGHEST) + b_ref[0]


def _mod(c, mod_w, mod_b):
    L, D, W = mod_w.shape
    B = c.shape[0]
    tn = _tile(W, 1536)
    return pl.pallas_call(
        _mod_kernel,
        out_shape=jax.ShapeDtypeStruct((L, B, W), F32),
        grid=(L, W // tn),
        in_specs=[
            pl.BlockSpec((B, D), lambda l, j: (0, 0)),
            pl.BlockSpec((1, D, tn), lambda l, j: (l, 0, j)),
            pl.BlockSpec((1, 1, tn), lambda l, j: (l, 0, j)),
        ],
        out_specs=pl.BlockSpec((1, B, tn), lambda l, j: (l, 0, j)),
        compiler_params=_cparams(("arbitrary", "arbitrary")),
        name="mod_proj",
    )(c, mod_w, mod_b.reshape(L, 1, W))


def _rope(v, c, sa, sb):
    return v * c + pltpu.roll(v, LANES - 16, 1) * sa + pltpu.roll(v, 16, 1) * sb


def _inproj_kernel(x_ref, sc_ref, sh_ref, g_ref, w_ref, ca_ref, saa_ref, sba_ref, cc_ref, sac_ref, sbc_ref,
                   gq_ref, gk_ref, gcq_ref, gckv_ref, wuq_ref, wuk_ref, wuv_ref,
                   qa_ref, ka_ref, va_ref, qb_ref, kb_ref, vb_ref, qc_ref, kc_ref, vc_ref, *, qs_ab, qs_c):
    x = x_ref[0]
    u = x * lax.rsqrt(jnp.mean(x * x, axis=-1, keepdims=True) + EPS) * g_ref[...]
    u = u * (1.0 + sc_ref[0]) + sh_ref[0]
    p = jnp.dot(u.astype(BF16), w_ref[...], preferred_element_type=F32)
    ca, saa, sba = ca_ref[...], saa_ref[...], sba_ref[...]
    cc, sac, sbc = cc_ref[...], sac_ref[...], sbc_ref[...]

    gq = gq_ref[...]
    for h in range(A_HEADS):
        v = p[:, OFF_AQ + LANES * h:OFF_AQ + LANES * (h + 1)]
        ss = jnp.sum(v * v, axis=-1, keepdims=True) * (1.0 / HEAD_DIM)
        v = v * lax.rsqrt(ss + EPS) * gq
        qa_ref[0, h] = (_rope(v, ca, saa, sba) * qs_ab).astype(BF16)
    k = p[:, OFF_AK:OFF_AK + LANES]
    lane = lax.broadcasted_iota(jnp.int32, k.shape, 1)
    k2 = k * k
    s0 = jnp.sum(jnp.where(lane < HEAD_DIM, k2, 0.0), axis=-1, keepdims=True)
    s1 = jnp.sum(k2, axis=-1, keepdims=True) - s0
    r = jnp.where(lane < HEAD_DIM, lax.rsqrt(s0 * (1.0 / HEAD_DIM) + EPS), lax.rsqrt(s1 * (1.0 / HEAD_DIM) + EPS))
    ka_ref[0] = _rope(k * r * gk_ref[...], ca, saa, sba).astype(BF16)
    ones = jnp.ones((ONES_ROWS, k.shape[0]), BF16)
    va_ref[0] = jnp.concatenate([p[:, OFF_AV:OFF_AV + LANES].T.astype(BF16), ones], axis=0)

    for h in range(B_HEADS):
        qb_ref[0, h] = (p[:, OFF_BQ + LANES * h:OFF_BQ + LANES * (h + 1)] * qs_ab).astype(BF16)
    kb_ref[0] = p[:, OFF_BK:OFF_BK + LANES].astype(BF16)
    vb_ref[0] = jnp.concatenate([p[:, OFF_BV:OFF_BV + LANES].T.astype(BF16), ones], axis=0)

    cq = p[:, OFF_CQ:OFF_CQ + C_Q_LORA]
    cq = cq * lax.rsqrt(jnp.mean(cq * cq, axis=-1, keepdims=True) + EPS) * gcq_ref[...]
    qh = jnp.dot(cq.astype(BF16), wuq_ref[...], preferred_element_type=F32)
    for h in range(C_HEADS):
        base = 2 * LANES * h
        qc_ref[0, h, :, 0:LANES] = (qh[:, base:base + LANES] * qs_c).astype(BF16)
        qc_ref[0, h, :, LANES:2 * LANES] = (_rope(qh[:, base + LANES:base + 2 * LANES], cc, sac, sbc) * qs_c).astype(BF16)
    ckv = p[:, OFF_CKV:OFF_CKV + C_KV_LORA]
    ckv = (ckv * lax.rsqrt(jnp.mean(ckv * ckv, axis=-1, keepdims=True) + EPS) * gckv_ref[...]).astype(BF16)
    kn = jnp.dot(ckv, wuk_ref[...], preferred_element_type=F32)
    vv = jnp.dot(ckv, wuv_ref[...], preferred_element_type=F32)
    kr = _rope(p[:, OFF_CKR:OFF_CKR + LANES], cc, sac, sbc).astype(BF16)
    for pr in range(C_HEADS // 2):
        kc_ref[0, pr, :, 0:LANES] = kn[:, LANES * pr:LANES * (pr + 1)].astype(BF16)
        kc_ref[0, pr, :, LANES:2 * LANES] = kr
        vc_ref[0, pr] = jnp.concatenate([vv[:, LANES * pr:LANES * (pr + 1)].T.astype(BF16), ones], axis=0)


def _inproj(x, sc1, sh1, g, w_pad, tabs_a, tabs_c, gq, gk, gcq, gckv, wuq_pad, wuk, wuv):
    B, S, D = x.shape
    tm = _tile(S, 512)
    full = lambda shp: pl.BlockSpec(shp, lambda b, i: (0,) * len(shp))
    tab = pl.BlockSpec((tm, LANES), lambda b, i: (i, 0))
    mod = pl.BlockSpec((1, 1, D), lambda b, i: (b, 0, 0))
    kern = functools.partial(_inproj_kernel, qs_ab=HEAD_DIM ** -0.5 * LOG2E, qs_c=(C_NOPE + C_ROPE) ** -0.5 * LOG2E)
    outs = (
        jax.ShapeDtypeStruct((B, A_HEADS, S, LANES), BF16), jax.ShapeDtypeStruct((B, S, LANES), BF16),
        jax.ShapeDtypeStruct((B, VT_ROWS, S), BF16),
        jax.ShapeDtypeStruct((B, B_HEADS, S, LANES), BF16), jax.ShapeDtypeStruct((B, S, LANES), BF16),
        jax.ShapeDtypeStruct((B, VT_ROWS, S), BF16),
        jax.ShapeDtypeStruct((B, C_HEADS, S, 2 * LANES), BF16), jax.ShapeDtypeStruct((B, C_HEADS // 2, S, 2 * LANES), BF16),
        jax.ShapeDtypeStruct((B, C_HEADS // 2, VT_ROWS, S), BF16),
    )
    slab = pl.BlockSpec((1, tm, LANES), lambda b, i: (b, i, 0))
    out_specs = (
        pl.BlockSpec((1, A_HEADS, tm, LANES), lambda b, i: (b, 0, i, 0)), slab,
        pl.BlockSpec((1, VT_ROWS, tm), lambda b, i: (b, 0, i)),
        pl.BlockSpec((1, B_HEADS, tm, LANES), lambda b, i: (b, 0, i, 0)), slab,
        pl.BlockSpec((1, VT_ROWS, tm), lambda b, i: (b, 0, i)),
        pl.BlockSpec((1, C_HEADS, tm, 2 * LANES), lambda b, i: (b, 0, i, 0)),
        pl.BlockSpec((1, C_HEADS // 2, tm, 2 * LANES), lambda b, i: (b, 0, i, 0)),
        pl.BlockSpec((1, C_HEADS // 2, VT_ROWS, tm), lambda b, i: (b, 0, 0, i)),
    )
    return pl.pallas_call(
        kern, out_shape=outs, grid=(B, S // tm),
        in_specs=[pl.BlockSpec((1, tm, D), lambda b, i: (b, i, 0)), mod, mod, full((1, D)), full((D, IN_PAD)),
                  tab, tab, tab, tab, tab, tab,
                  full((1, LANES)), full((1, LANES)), full((1, C_Q_LORA)), full((1, C_KV_LORA)),
                  full(wuq_pad.shape), full(wuk.shape), full(wuv.shape)],
        out_specs=out_specs,
        compiler_params=_cparams(("arbitrary", "arbitrary")),
        name="norm_inproj",
    )(x, sc1, sh1, g, w_pad, *tabs_a, *tabs_c, gq, gk, gcq, gckv, wuq_pad, wuk, wuv)


def _flash_kernel(q_ref, k_ref, vt_ref, e_ref, o_ref, qt_ref, m_ref, acc_ref, *, tk, cw, unroll):
    H, tq, dk = q_ref.shape[2:]
    S = k_ref.shape[2]
    R = H * tq
    for h in range(H):
        qt_ref[:, h * tq:(h + 1) * tq] = q_ref[0, 0, h].astype(F32).T.astype(BF16)
    m_ref[...] = jnp.full(m_ref.shape, NEG_INF, F32)
    acc_ref[...] = jnp.zeros(acc_ref.shape, F32)

    def body(j, carry):
        off = pl.multiple_of(j * tk, tk)
        k = k_ref[0, 0, pl.ds(off, tk), :]
        vt = vt_ref[0, 0, :, pl.ds(off, tk)]
        nc = R // cw
        scores = lambda c: jnp.dot(k, qt_ref[:, c * cw:(c + 1) * cw], preferred_element_type=F32)
        accs, ms = [], []
        pending = [scores(c) for c in range(min(LOOKAHEAD, nc))]
        for c in range(nc):
            sl = slice(c * cw, (c + 1) * cw)
            s = pending.pop(0)
            if c + LOOKAHEAD < nc:
                pending.append(scores(c + LOOKAHEAD))
            m_prev = m_ref[:, sl]
            m_new = jnp.maximum(m_prev, jnp.max(s, axis=0, keepdims=True))
            p = jnp.exp2(s - m_new).astype(BF16)
            alpha = jnp.exp2(m_prev - m_new)
            accs.append(alpha * acc_ref[:, sl] + jnp.dot(vt, p, preferred_element_type=F32))
            ms.append(m_new)
        acc_ref[...] = jnp.concatenate(accs, axis=1)
        m_ref[...] = jnp.concatenate(ms, axis=1)
        return carry

    lax.fori_loop(0, S // tk, body, 0, unroll=unroll)
    o_t = acc_ref[0:LANES, :] / acc_ref[LANES:LANES + 1, :]
    out = None
    for h in range(H):
        o_h = o_t[:, h * tq:(h + 1) * tq].T.astype(BF16)
        term = jnp.dot(o_h, e_ref[h], preferred_element_type=F32)
        out = term if out is None else out + term
    o_ref[0] = out.astype(BF16)


def _flash(q, k, vt, esel, *, tq, tk, cw=256, unroll=2):
    B, P, H, S, dk = q.shape
    W = esel.shape[2]
    tq = _tile(S, tq)
    tk = _tile(S, tk)
    R = H * tq
    cw = _tile(R, cw)
    return pl.pallas_call(
        functools.partial(_flash_kernel, tk=tk, cw=cw, unroll=unroll),
        out_shape=jax.ShapeDtypeStruct((B, S, P * W), BF16),
        grid=(B, P, S // tq),
        in_specs=[pl.BlockSpec((1, 1, H, tq, dk), lambda b, p, i: (b, p, 0, i, 0)),
                  pl.BlockSpec((1, 1, S, dk), lambda b, p, i: (b, p, 0, 0)),
                  pl.BlockSpec((1, 1, VT_ROWS, S), lambda b, p, i: (b, p, 0, 0)),
                  pl.BlockSpec((H, LANES, W), lambda b, p, i: (0, 0, 0))],
        out_specs=pl.BlockSpec((1, tq, W), lambda b, p, i: (b, i, p)),
        scratch_shapes=[pltpu.VMEM((dk, R), BF16), pltpu.VMEM((1, R), F32), pltpu.VMEM((VT_ROWS, R), F32)],
        compiler_params=_cparams(("arbitrary", "arbitrary", "arbitrary")),
        name="flash_attn",
    )(q, k, vt, esel)


def _window_kernel(q_ref, k_ref, vt_ref, e_ref, slope_ref, sink_ref, o_ref, *, tq):
    H = q_ref.shape[1]
    S = k_ref.shape[1]
    wk = min(S, tq + 2 * WINDOW)
    i = pl.program_id(1)
    start = pl.multiple_of(jnp.clip(i * tq - WINDOW, 0, S - wk), WINDOW)
    k = k_ref[0, pl.ds(start, wk), :]
    vt = vt_ref[0, :, pl.ds(start, wk)]
    spos = start + lax.broadcasted_iota(jnp.int32, (wk, tq), 0)
    tpos = i * tq + lax.broadcasted_iota(jnp.int32, (wk, tq), 1)
    dist = jnp.abs(spos - tpos)
    inside = dist <= WINDOW
    distf = dist.astype(F32)
    qt = jnp.concatenate([q_ref[0, h].astype(F32).T.astype(BF16) for h in range(H)], axis=1)
    s_all = jnp.dot(k, qt, preferred_element_type=F32)
    es, ms = [], []
    for h in range(H):
        s = s_all[:, h * tq:(h + 1) * tq] - slope_ref[h] * distf
        s = jnp.where(inside, s, NEG_INF)
        m = jnp.maximum(jnp.max(s, axis=0, keepdims=True), sink_ref[h])
        es.append(jnp.exp2(s - m).astype(BF16))
        ms.append(m)
    acc = jnp.dot(vt, jnp.concatenate(es, axis=1), preferred_element_type=F32)
    out = None
    for h in range(H):
        sl = slice(h * tq, (h + 1) * tq)
        den = acc[LANES:LANES + 1, sl] + jnp.exp2(sink_ref[h] - ms[h])
        o = (acc[0:LANES, sl] / den).T.astype(BF16)
        term = jnp.dot(o, e_ref[h], preferred_element_type=F32)
        out = term if out is None else out + term
    o_ref[0] = out.astype(BF16)


def _window(q, k, v, esel, slopes, sink, *, tq):
    B, H, S, _ = q.shape
    W = esel.shape[2]
    tq = _tile(S, tq)
    return pl.pallas_call(
        functools.partial(_window_kernel, tq=tq),
        out_shape=jax.ShapeDtypeStruct((B, S, W), BF16),
        grid=(B, S // tq),
        in_specs=[pl.BlockSpec((1, H, tq, LANES), lambda b, i: (b, 0, i, 0)),
                  pl.BlockSpec((1, S, LANES), lambda b, i: (b, 0, 0)),
                  pl.BlockSpec((1, VT_ROWS, S), lambda b, i: (b, 0, 0)),
                  pl.BlockSpec((H, LANES, W), lambda b, i: (0, 0, 0)),
                  pl.BlockSpec((H, 1, 1), lambda b, i: (0, 0, 0)),
                  pl.BlockSpec((H, 1, 1), lambda b, i: (0, 0, 0))],
        out_specs=pl.BlockSpec((1, tq, W), lambda b, i: (b, i, 0)),
        compiler_params=_cparams(("arbitrary", "arbitrary")),
        name="window_attn",
    )(q, k, v, esel, slopes, sink)


def _sigmoid(v):
    return 1.0 / (1.0 + jnp.exp(-v))


def _pack_token_words(x):
    rp = x.shape[1] // (2 * LANES)
    words = []
    for r in range(rp):
        lo = lax.bitcast_convert_type(x[:, LANES * r:LANES * (r + 1)].astype(BF16).astype(F32), jnp.uint32)
        hi = lax.bitcast_convert_type(x[:, LANES * (rp + r):LANES * (rp + r + 1)].astype(BF16).astype(F32), jnp.uint32)
        words.append((lo >> 16) | hi)
    return words


def _load_token_words(ref, row0, n, rp):
    return [ref[pl.ds(row0 + r, n, stride=rp), :] for r in range(rp)]


def _unpack_token_words(words):
    lo = [lax.bitcast_convert_type(w << 16, F32) for w in words]
    hi = [lax.bitcast_convert_type(w & jnp.uint32(0xFFFF0000), F32) for w in words]
    return jnp.concatenate(lo + hi, axis=1)


def _post_kernel(oa_ref, ob_ref, oc_ref, x_ref, g1_ref, sc2_ref, sh2_ref, g2_ref, og_ref, wout_ref, n2_ref, rwt_ref,
                 sw1_ref, sw3_ref, sw2_ref, xs1_ref, u2_ref, st_ref):
    og = og_ref[...]

    def rn(o_ref, g):
        o = o_ref[0].astype(F32)
        return (o * lax.rsqrt(jnp.mean(o * o, axis=-1, keepdims=True) + EPS) * g).astype(BF16)

    mixed = jnp.concatenate([rn(oa_ref, og[:, :A_WIDTH]), rn(ob_ref, og[:, A_WIDTH:A_WIDTH + B_WIDTH]),
                             rn(oc_ref, og[:, A_WIDTH + B_WIDTH:])], axis=-1)
    x1 = x_ref[0] + g1_ref[0] * jnp.dot(mixed, wout_ref[...], preferred_element_type=F32)
    u2 = x1 * lax.rsqrt(jnp.mean(x1 * x1, axis=-1, keepdims=True) + EPS) * n2_ref[...]
    u2 = u2 * (1.0 + sc2_ref[0]) + sh2_ref[0]
    words = _pack_token_words(u2)
    for r, w in enumerate(words):
        u2_ref[pl.ds(r, u2.shape[0], stride=len(words)), :] = w
    ub = u2.astype(BF16)
    logits_t = lax.dot_general(rwt_ref[...], ub, (((1,), (1,)), ((), ())), preferred_element_type=F32)
    st_ref[...] = _sigmoid(logits_t)
    h1 = jnp.dot(ub, sw1_ref[...], preferred_element_type=F32)
    h3 = jnp.dot(ub, sw3_ref[...], preferred_element_type=F32)
    hid = (h1 * _sigmoid(h1) * h3).astype(BF16)
    xs1_ref[0] = x1 + g2_ref[0] * jnp.dot(hid, sw2_ref[...], preferred_element_type=F32)


def _post(oa, ob, oc, x, g1, sc2, sh2, g2, og, wout, n2, rwt, sw1, sw3, sw2):
    B, S, D = x.shape
    E = rwt.shape[0]
    tm = _tile(S, 512)
    nt = S // tm
    full = lambda a: pl.BlockSpec(a.shape, lambda b, i: (0,) * a.ndim)
    mod = pl.BlockSpec((1, 1, D), lambda b, i: (b, 0, 0))
    row = lambda w: pl.BlockSpec((1, tm, w), lambda b, i: (b, i, 0))
    return pl.pallas_call(
        _post_kernel,
        out_shape=(jax.ShapeDtypeStruct((B, S, D), F32),
                   jax.ShapeDtypeStruct((B * S * D // (2 * LANES), LANES), jnp.uint32),
                   jax.ShapeDtypeStruct((E, B * S), F32)),
        grid=(B, nt),
        in_specs=[row(A_WIDTH), row(B_WIDTH), row(C_WIDTH), row(D), mod, mod, mod, mod, full(og), full(wout), full(n2),
                  full(rwt), full(sw1), full(sw3), full(sw2)],
        out_specs=(row(D), pl.BlockSpec((tm * D // (2 * LANES), LANES), lambda b, i: (b * nt + i, 0)),
                   pl.BlockSpec((E, tm), lambda b, i: (0, b * nt + i))),
        compiler_params=_cparams(("arbitrary", "arbitrary")),
        name="post_attn",
    )(oa, ob, oc, x, g1, sc2, sh2, g2, og, wout, n2, rwt, sw1, sw3, sw2)


def _route_kernel(st_ref, rb_ref, tri_ref, idx_ref, w_ref, rank_ref, cnt_ref, carry_ref):
    E, tm = st_ref.shape
    per = E // N_GROUPS
    big = float(E)

    @pl.when(pl.program_id(0) == 0)
    def _():
        carry_ref[...] = jnp.zeros(carry_ref.shape, F32)

    s = st_ref[...]
    choice = s + rb_ref[...]
    c3 = choice.reshape(N_GROUPS, per, tm)
    io3 = lax.broadcasted_iota(jnp.int32, c3.shape, 1).astype(F32)
    m1 = jnp.max(c3, axis=1, keepdims=True)
    i1 = jnp.min(jnp.where(c3 == m1, io3, big), axis=1, keepdims=True)
    m2 = jnp.max(jnp.where(io3 == i1, -jnp.inf, c3), axis=1, keepdims=True)
    gs = (m1 + m2).reshape(N_GROUPS, tm)
    gio = lax.broadcasted_iota(jnp.int32, gs.shape, 0).astype(F32)
    keep = jnp.zeros(gs.shape, F32)
    for _ in range(TOPK_GROUPS):
        mg = jnp.max(gs, axis=0, keepdims=True)
        ig = jnp.min(jnp.where(gs == mg, gio, big), axis=0, keepdims=True)
        sel = gio == ig
        keep = jnp.where(sel, 1.0, keep)
        gs = jnp.where(sel, -jnp.inf, gs)
    cur = jnp.where(keep.reshape(N_GROUPS, 1, tm) > 0.5, c3, NEG_INF).reshape(E, tm)
    eio = lax.broadcasted_iota(jnp.int32, (E, tm), 0).astype(F32)
    sels, idxs, ws = [], [], []
    for _ in range(TOP_K):
        m = jnp.max(cur, axis=0, keepdims=True)
        ik = jnp.min(jnp.where(cur == m, eio, big), axis=0, keepdims=True)
        sel = eio == ik
        ws.append(jnp.sum(jnp.where(sel, s, 0.0), axis=0, keepdims=True))
        cur = jnp.where(sel, -jnp.inf, cur)
        sels.append(sel)
        idxs.append(ik)
    wsum = ws[0]
    for wk in ws[1:]:
        wsum = wsum + wk
    w_ref[...] = jnp.concatenate([wk / wsum * ROUTED_SCALE for wk in ws], axis=0)
    idx_ref[...] = jnp.concatenate(idxs, axis=0).astype(jnp.int32)
    onehot = jnp.zeros((E, tm), F32)
    for sel in sels:
        onehot = jnp.where(sel, 1.0, onehot)
    prefix = jnp.dot(onehot.astype(BF16), tri_ref[...], preferred_element_type=F32) + carry_ref[...]
    rank_ref[...] = jnp.concatenate(
        [jnp.sum(jnp.where(sel, prefix, 0.0), axis=0, keepdims=True) for sel in sels], axis=0).astype(jnp.int32)
    carry = carry_ref[...] + jnp.sum(onehot, axis=1, keepdims=True)
    carry_ref[...] = carry
    cnt_ref[...] = jnp.broadcast_to(carry, cnt_ref.shape)


def _route(scores_t, router_b):
    E, N = scores_t.shape
    tm = _tile(N, 512)
    tri = (jnp.arange(tm)[:, None] < jnp.arange(tm)[None, :]).astype(BF16)
    return pl.pallas_call(
        _route_kernel,
        out_shape=(jax.ShapeDtypeStruct((TOP_K, N), jnp.int32), jax.ShapeDtypeStruct((TOP_K, N), F32),
                   jax.ShapeDtypeStruct((TOP_K, N), jnp.int32), jax.ShapeDtypeStruct((E, LANES), F32)),
        grid=(N // tm,),
        in_specs=[pl.BlockSpec((E, tm), lambda i: (0, i)), pl.BlockSpec((E, 1), lambda i: (0, 0)),
                  pl.BlockSpec((tm, tm), lambda i: (0, 0))],
        out_specs=(pl.BlockSpec((TOP_K, tm), lambda i: (0, i)), pl.BlockSpec((TOP_K, tm), lambda i: (0, i)),
                   pl.BlockSpec((TOP_K, tm), lambda i: (0, i)), pl.BlockSpec((E, LANES), lambda i: (0, 0))),
        scratch_shapes=[pltpu.VMEM((E, 1), F32)],
        compiler_params=_cparams(("arbitrary",)),
        name="route_topk",
    )(scores_t, router_b.reshape(E, 1), tri)


def _slots_kernel(idx_ref, rank_ref, starts_ref, dest_ref):
    E = starts_ref.shape[0]
    tm = idx_ref.shape[1]
    eio = lax.broadcasted_iota(jnp.int32, (E, tm), 0)
    starts = starts_ref[...]
    rows = [jnp.sum(jnp.where(eio == idx_ref[k:k + 1, :], starts, 0.0), axis=0, keepdims=True) for k in range(TOP_K)]
    dest_ref[...] = jnp.concatenate(rows, axis=0).astype(jnp.int32) + rank_ref[...]


def _slots(idx, rank, starts):
    K, N = idx.shape
    E = starts.shape[0]
    tm = _tile(N, 512)
    blk = pl.BlockSpec((K, tm), lambda i: (0, i))
    return pl.pallas_call(
        _slots_kernel, out_shape=jax.ShapeDtypeStruct((K, N), jnp.int32), grid=(N // tm,),
        in_specs=[blk, blk, pl.BlockSpec((E, 1), lambda i: (0, 0))], out_specs=blk,
        compiler_params=_cparams(("arbitrary",)), name="moe_slots",
    )(idx, rank, starts.astype(F32).reshape(E, 1))


TOKENS_PER_ISSUE = 8


def _dispatch_kernel(dest_ref, u_ref, xs_ref, sem, *, ch):
    tm = u_ref.shape[0] // ch

    def body(g, carry):
        base = pl.multiple_of(g * (TOKENS_PER_ISSUE * ch), TOKENS_PER_ISSUE * ch)
        for r in range(TOKENS_PER_ISSUE):
            for k in range(TOP_K):
                d = dest_ref[g * (TOKENS_PER_ISSUE * TOP_K) + r * TOP_K + k]
                pltpu.make_async_copy(u_ref.at[pl.ds(base + r * ch, ch)],
                                      xs_ref.at[pl.ds(pl.multiple_of(d * ch, ch), ch)],
                                      sem).start(priority=k % 2)
        return carry

    lax.fori_loop(0, tm // TOKENS_PER_ISSUE, body, 0)
    for k in range(TOP_K):
        pltpu.make_async_copy(u_ref, xs_ref.at[pl.ds(0, tm * ch)], sem).wait()


def _dispatch(dest_flat, u2t, n_slots, ch):
    N = u2t.shape[0] // ch
    tm = _tile(N, 256)
    return pl.pallas_call(
        functools.partial(_dispatch_kernel, ch=ch),
        out_shape=jax.ShapeDtypeStruct((n_slots * ch, LANES), u2t.dtype),
        grid=(N // tm,),
        in_specs=[pl.BlockSpec((tm * TOP_K,), lambda i: (i,), memory_space=pltpu.SMEM),
                  pl.BlockSpec((tm * ch, LANES), lambda i: (i, 0))],
        out_specs=pl.BlockSpec(memory_space=pl.ANY),
        scratch_shapes=[pltpu.SemaphoreType.DMA(())],
        compiler_params=pltpu.CompilerParams(dimension_semantics=("arbitrary",), vmem_limit_bytes=VMEM_LIMIT,
                                             has_side_effects=True),
        name="moe_dispatch",
    )(dest_flat, u2t)


def _expert_kernel(wb_ref, we_ref, lo_ref, hi_ref, nw_ref, xs_ref, w1_ref, w3_ref, w2_ref, ys_ref, w1b, w3b, w2b):
    j = pl.program_id(0)

    @pl.when(j < nw_ref[0])
    def _():
        prev = jnp.maximum(j - 1, 0)
        new_expert = jnp.logical_or(j == 0, we_ref[j] != we_ref[prev])
        new_block = jnp.logical_or(j == 0, wb_ref[j] != wb_ref[prev])

        @pl.when(new_expert)
        def _():
            w1b[...] = w1_ref[0].astype(BF16)
            w3b[...] = w3_ref[0].astype(BF16)
            w2b[...] = w2_ref[0].astype(BF16)

        rp = w1b.shape[0] // (2 * LANES)
        rows = lax.broadcasted_iota(jnp.int32, (MOE_BM, LANES), 0)
        mine = (rows >= lo_ref[j]) & (rows < hi_ref[j])
        words = [[] for _ in range(rp)]
        for s in range(MOE_SPLIT):
            r0, nr = s * (MOE_BM // MOE_SPLIT), MOE_BM // MOE_SPLIT
            xw = [jnp.where(mine[r0:r0 + nr], w, jnp.uint32(0)) for w in _load_token_words(xs_ref, r0 * rp, nr, rp)]
            xb = _unpack_token_words(xw).astype(BF16)
            h1 = jnp.dot(xb, w1b[...], preferred_element_type=F32)
            h3 = jnp.dot(xb, w3b[...], preferred_element_type=F32)
            hid = (h1 * _sigmoid(h1) * h3).astype(BF16)
            for r, w in enumerate(_pack_token_words(jnp.dot(hid, w2b[...], preferred_element_type=F32))):
                words[r].append(w)
        words = [jnp.concatenate(ws, axis=0) for ws in words]

        @pl.when(new_block)
        def _():
            for r, w in enumerate(words):
                ys_ref[pl.ds(r, MOE_BM, stride=rp), :] = w

        @pl.when(jnp.logical_not(new_block))
        def _():
            for r, prev in enumerate(_load_token_words(ys_ref, 0, MOE_BM, rp)):
                ys_ref[pl.ds(r, MOE_BM, stride=rp), :] = jnp.where(mine, words[r], prev)


def _experts(wb, we, lo, hi, nw, xs, w1, w3, w2, *, layer):
    D, F = w1.shape[2:]
    rows = MOE_BM * D // (2 * LANES)
    blk = lambda j, wb, we, lo, hi, nw: (wb[j], 0)
    wsel = lambda j, wb, we, lo, hi, nw: (layer, we[j], 0, 0)
    grid_spec = pltpu.PrefetchScalarGridSpec(
        num_scalar_prefetch=5, grid=(wb.shape[0],),
        in_specs=[pl.BlockSpec((rows, LANES), blk), pl.BlockSpec((None, 1, D, F), wsel),
                  pl.BlockSpec((None, 1, D, F), wsel), pl.BlockSpec((None, 1, F, D), wsel)],
        out_specs=pl.BlockSpec((rows, LANES), blk),
        scratch_shapes=[pltpu.VMEM((D, F), BF16), pltpu.VMEM((D, F), BF16), pltpu.VMEM((F, D), BF16)])
    return pl.pallas_call(
        _expert_kernel, out_shape=jax.ShapeDtypeStruct(xs.shape, xs.dtype), grid_spec=grid_spec,
        compiler_params=_cparams(("arbitrary",)), name="moe_experts",
    )(wb, we, lo, hi, nw, xs, w1, w3, w2)


def _combine_kernel(dcur_ref, dnext_ref, w_ref, xs1_ref, g2_ref, fg_ref, ys_ref, o_ref, buf, sems, *, final):
    tm, D = xs1_ref.shape
    ch = D // (2 * LANES)
    i = pl.program_id(0)
    n = pl.num_programs(0)

    def issue(dref, slot):
        def body(g, carry):
            base = pl.multiple_of(g * (TOKENS_PER_ISSUE * ch), TOKENS_PER_ISSUE * ch)
            for r in range(TOKENS_PER_ISSUE):
                for k in range(TOP_K):
                    d = dref[g * (TOKENS_PER_ISSUE * TOP_K) + r * TOP_K + k]
                    pltpu.make_async_copy(ys_ref.at[pl.ds(pl.multiple_of(d * ch, ch), ch)],
                                          buf.at[slot, k, pl.ds(base + r * ch, ch)],
                                          sems.at[slot]).start(priority=k % 2)
            return carry

        lax.fori_loop(0, tm // TOKENS_PER_ISSUE, body, 0)

    def finish(slot):
        for k in range(TOP_K):
            pltpu.make_async_copy(ys_ref.at[pl.ds(0, tm * ch)], buf.at[slot, k], sems.at[slot]).wait()
        w = w_ref[...]
        r = None
        for k in range(TOP_K):
            term = w[:, k:k + 1] * _unpack_token_words(_load_token_words(buf.at[slot, k], 0, tm, ch))
            r = term if r is None else r + term
        x2 = xs1_ref[...] + g2_ref[0] * r
        if final:
            x2 = x2 * lax.rsqrt(jnp.mean(x2 * x2, axis=-1, keepdims=True) + EPS) * fg_ref[...]
        o_ref[...] = x2

    @pl.when(i == 0)
    def _():
        issue(dcur_ref, 0)

    for slot in range(2):
        @pl.when(jnp.logical_and(i + 1 < n, (i + 1) % 2 == slot))
        def _():
            issue(dnext_ref, slot)

    for slot in range(2):
        @pl.when(i % 2 == slot)
        def _():
            finish(slot)


def _combine(dest_flat, w_tok, xs1, g2, fg, ys, *, seq, final):
    N, D = xs1.shape
    ch = D // (2 * LANES)
    tm = _tile(seq, 128)
    per_b = seq // tm
    n = N // tm
    return pl.pallas_call(
        functools.partial(_combine_kernel, final=final),
        out_shape=jax.ShapeDtypeStruct((N, D), F32),
        grid=(n,),
        in_specs=[pl.BlockSpec((tm * TOP_K,), lambda i: (i,), memory_space=pltpu.SMEM),
                  pl.BlockSpec((tm * TOP_K,), lambda i: (jnp.minimum(i + 1, n - 1),), memory_space=pltpu.SMEM),
                  pl.BlockSpec((tm, TOP_K), lambda i: (i, 0)),
                  pl.BlockSpec((tm, D), lambda i: (i, 0)),
                  pl.BlockSpec((1, 1, D), lambda i: (i // per_b, 0, 0)),
                  pl.BlockSpec((1, D), lambda i: (0, 0)),
                  pl.BlockSpec(memory_space=pl.ANY)],
        out_specs=pl.BlockSpec((tm, D), lambda i: (i, 0)),
        scratch_shapes=[pltpu.VMEM((2, TOP_K, tm * ch, LANES), jnp.uint32), pltpu.SemaphoreType.DMA((2,))],
        compiler_params=_cparams(("arbitrary",)),
        name="moe_combine",
    )(dest_flat, dest_flat, w_tok, xs1, g2, fg, ys)


def _rope_tables(S):
    half = HEAD_DIM // 2
    inv = ROPE_THETA ** (-jnp.arange(0, half, 2, dtype=F32) / half)
    t = jnp.arange(S, dtype=jnp.int32)

    def part(pos):
        ang = pos.astype(F32)[:, None] * inv[None, :]
        c, s, z = jnp.cos(ang), jnp.sin(ang), jnp.zeros_like(ang)
        return jnp.concatenate([c, c], -1), jnp.concatenate([-s, z], -1), jnp.concatenate([z, s], -1)

    row, col, lin = part(t // GRID_W), part(t % GRID_W), part(t)
    tabs_a = tuple(jnp.tile(jnp.concatenate([r, c], -1), (1, LANES // HEAD_DIM)) for r, c in zip(row, col))
    tabs_c = tuple(jnp.pad(v, ((0, 0), (0, LANES - C_ROPE))) for v in lin)
    return tabs_a, tabs_c


def _head_slabs(w, heads, groups):
    d = w.shape[0]
    hot = (jnp.arange(heads)[:, None] // (heads // groups) == jnp.arange(groups)[None, :]).astype(w.dtype)
    return (w.reshape(d, heads, 1, HEAD_DIM) * hot[None, :, :, None]).reshape(d, heads * LANES)


def _pad_w_in(w):
    d = w.shape[0]
    a, b, c = w[:, :A_IN], w[:, A_IN:A_IN + B_IN], w[:, A_IN + B_IN:]
    kr = jnp.pad(c[:, C_Q_LORA + C_KV_LORA:], ((0, 0), (0, LANES - C_ROPE)))
    return jnp.concatenate([_head_slabs(a[:, :A_WIDTH], A_HEADS, A_KV_HEADS), a[:, A_WIDTH:],
                            _head_slabs(b[:, :B_WIDTH], B_HEADS, B_KV_HEADS), b[:, B_WIDTH:],
                            c[:, :C_Q_LORA + C_KV_LORA], kr], axis=1).astype(BF16)


def _pad_w_uq(w):
    r = w.shape[0]
    w = w.reshape(r, C_HEADS, C_NOPE + C_ROPE)
    hot = (jnp.arange(C_HEADS)[:, None] % 2 == jnp.arange(2)[None, :]).astype(w.dtype)
    nope = (w[:, :, None, :C_NOPE] * hot[None, :, :, None]).reshape(r, C_HEADS, LANES)
    rope = jnp.pad(w[:, :, C_NOPE:], ((0, 0), (0, 0), (0, LANES - C_ROPE)))
    return jnp.concatenate([nope, rope], axis=-1).reshape(r, C_HEADS * 2 * LANES).astype(BF16)


def _select_mats(heads, groups, out_w):
    h = jnp.arange(heads)[:, None, None]
    r = jnp.arange(LANES)[None, :, None]
    c = jnp.arange(out_w)[None, None, :]
    half = h // (heads // groups)
    return ((r // HEAD_DIM == half) & (c == HEAD_DIM * h + r % HEAD_DIM)).astype(BF16)


def kernel(x, c, mod_w, mod_b, norm1_g, norm2_g, w_in, a_qnorm_g, a_knorm_g, b_sink, c_qnorm_g, c_kvnorm_g, c_w_uq, c_w_uk, c_w_uv, out_norm_g, w_out, router_w, router_b, exp_w1, exp_w3, exp_w2, sh_w1, sh_w3, sh_w2, final_g):
    B, S, D = x.shape
    L = mod_w.shape[0]
    E = router_w.shape[2]
    N = B * S
    n_slots = N * TOP_K
    assert n_slots % MOE_BM == 0
    n_work = n_slots // MOE_BM + E

    mod = _mod(c, mod_w, mod_b).reshape(L, B, 6, 1, D)
    tabs_a, tabs_c = _rope_tables(S)
    esel_a = _select_mats(A_HEADS, A_KV_HEADS, A_WIDTH)
    esel_c = _select_mats(2, 2, LANES)
    slopes = (jnp.exp2(-8.0 * jnp.arange(1, B_HEADS + 1, dtype=F32) / B_HEADS) * LOG2E).reshape(B_HEADS, 1, 1)

    for l in range(L):
        sh1, sc1, g1, sh2, sc2, g2 = [mod[l, :, i] for i in range(6)]
        gq2 = jnp.tile(a_qnorm_g[l], LANES // HEAD_DIM).reshape(1, LANES)
        gk2 = jnp.tile(a_knorm_g[l], LANES // HEAD_DIM).reshape(1, LANES)
        qa, ka, va, qb, kb, vb, qc, kc, vc = _inproj(
            x, sc1, sh1, norm1_g[l].reshape(1, D), _pad_w_in(w_in[l]), tabs_a, tabs_c, gq2, gk2,
            c_qnorm_g[l].reshape(1, -1), c_kvnorm_g[l].reshape(1, -1), _pad_w_uq(c_w_uq[l]),
            c_w_uk[l].astype(BF16), c_w_uv[l].astype(BF16))
        oa = _flash(qa[:, None], ka[:, None], va[:, None], esel_a, tq=1024, tk=512, cw=512, unroll=1)
        ob = _window(qb, kb, vb, esel_a, slopes, (b_sink[l].astype(F32) * LOG2E).reshape(B_HEADS, 1, 1), tq=256)
        oc = _flash(qc.reshape(B, C_HEADS // 2, 2, S, 2 * LANES), kc, vc, esel_c, tq=2048, tk=512, cw=512, unroll=1)
        xs1, u2, scores_t = _post(
            oa, ob, oc, x, g1, sc2, sh2, g2, out_norm_g[l].reshape(1, -1), w_out[l].astype(BF16),
            norm2_g[l].reshape(1, D), router_w[l].T.astype(BF16), sh_w1[l].astype(BF16), sh_w3[l].astype(BF16),
            sh_w2[l].astype(BF16))
        idx, w_top, rank, cnt = _route(scores_t, router_b[l])
        counts = cnt[:, 0].astype(jnp.int32)
        ends = jnp.cumsum(counts)
        starts = ends - counts
        dest = _slots(idx, rank, starts).T.reshape(N * TOP_K)
        first_blk = starts // MOE_BM
        n_items = jnp.where(counts > 0, (ends - 1) // MOE_BM - first_blk + 1, 0)
        item_ends = jnp.cumsum(n_items)
        nw = item_ends[-1].reshape(1).astype(jnp.int32)
        wid = jnp.minimum(jnp.arange(n_work, dtype=jnp.int32), nw[0] - 1)
        we = jnp.minimum(jnp.sum(item_ends[None, :] <= wid[:, None], axis=1), E - 1).astype(jnp.int32)
        hot = we[:, None] == jnp.arange(E, dtype=jnp.int32)[None, :]
        take = lambda tbl: jnp.sum(jnp.where(hot, tbl[None, :], 0), axis=1).astype(jnp.int32)
        wb = take(first_blk) + wid - take(item_ends - n_items)
        lo = jnp.clip(take(starts) - wb * MOE_BM, 0, MOE_BM).astype(jnp.int32)
        hi = jnp.clip(take(ends) - wb * MOE_BM, 0, MOE_BM).astype(jnp.int32)
        xs = _dispatch(dest, u2, n_slots, D // (2 * LANES))
        ys = _experts(wb, we, lo, hi, nw, xs, exp_w1, exp_w3, exp_w2, layer=l)
        x = _combine(dest, w_top.T, xs1.reshape(N, D), g2, final_g.reshape(1, D), ys, seq=S,
                     final=(l == L - 1)).reshape(B, S, D)
    return x
```

```python
import functools

import jax
import jax.numpy as jnp
from jax import lax
from jax.experimental import pallas as pl
from jax.experimental.pallas import tpu as pltpu

F32 = jnp.float32
BF16 = jnp.bfloat16

HEAD_DIM = 64
GRID_W = 64
ROPE_THETA = 10000.0
EPS = 1e-6
NEG_INF = -1e30
A_HEADS, A_KV_HEADS = 6, 2
B_HEADS, B_KV_HEADS = 6, 2
WINDOW = 128
C_HEADS, C_Q_LORA, C_KV_LORA, C_NOPE, C_ROPE, C_V = 4, 256, 128, 64, 32, 64
A_WIDTH = A_HEADS * HEAD_DIM
B_WIDTH = B_HEADS * HEAD_DIM
C_WIDTH = C_HEADS * C_V
A_IN = (A_HEADS + 2 * A_KV_HEADS) * HEAD_DIM
B_IN = (B_HEADS + 2 * B_KV_HEADS) * HEAD_DIM
TOP_K = 8
N_GROUPS = 8
TOPK_GROUPS = 4
ROUTED_SCALE = 2.5
LOG2E = 1.4426950408889634

LANES = 128
SUBLANE_BITS = 3
SUBLANES = 1 << SUBLANE_BITS
OFF_AQ = 0
OFF_AK = OFF_AQ + A_HEADS * LANES
OFF_AV = OFF_AK + LANES
OFF_BQ = OFF_AV + LANES
OFF_BK = OFF_BQ + B_HEADS * LANES
OFF_BV = OFF_BK + LANES
OFF_CQ = OFF_BV + LANES
OFF_CKV = OFF_CQ + C_Q_LORA
OFF_CKR = OFF_CKV + C_KV_LORA
IN_PAD = OFF_CKR + LANES

ONES_ROWS = 16
VT_ROWS = LANES + ONES_ROWS
LOOKAHEAD = 3
MAX_JUMP = 64.0
MOE_BM = 512
MOE_SPLIT = 2
VMEM_LIMIT = 56 * 1024 * 1024


def _tile(n, pref):
    t = min(n, pref)
    assert n % t == 0, (n, t)
    return t


def _cparams(sem, vmem=VMEM_LIMIT):
    return pltpu.CompilerParams(dimension_semantics=sem, vmem_limit_bytes=vmem)


def _mod_kernel(c_ref, w_ref, b_ref, o_ref):
    c = c_ref[...]
    ca = c * (1.0 / (1.0 + jnp.exp(-c)))
    o_ref[0] = jnp.dot(ca, w_ref[0], preferred_element_type=F32, precision=lax.Precision.HIGHEST) + b_ref[0]


def _mod(c, mod_w, mod_b):
    L, D, W = mod_w.shape
    B = c.shape[0]
    tn = _tile(W, 1536)
    return pl.pallas_call(
        _mod_kernel,
        out_shape=jax.ShapeDtypeStruct((L, B, W), F32),
        grid=(L, W // tn),
        in_specs=[
            pl.BlockSpec((B, D), lambda l, j: (0, 0)),
            pl.BlockSpec((1, D, tn), lambda l, j: (l, 0, j)),
            pl.BlockSpec((1, 1, tn), lambda l, j: (l, 0, j)),
        ],
        out_specs=pl.BlockSpec((1, B, tn), lambda l, j: (l, 0, j)),
        compiler_params=_cparams(("arbitrary", "arbitrary")),
        name="mod_proj",
    )(c, mod_w, mod_b.reshape(L, 1, W))


def _rope(v, c, sa, sb):
    return v * c + pltpu.roll(v, LANES - 16, 1) * sa + pltpu.roll(v, 16, 1) * sb


def _inproj_kernel(x_ref, sc_ref, sh_ref, g_ref, w_ref, ca_ref, saa_ref, sba_ref, cc_ref, sac_ref, sbc_ref,
                   gq_ref, gk_ref, gcq_ref, gckv_ref, wuq_ref, wuk_ref, wuv_ref,
                   qa_ref, ka_ref, va_ref, qb_ref, kb_ref, vb_ref, qc_ref, kc_ref, vc_ref, *, qs_ab, qs_c):
    x = x_ref[0]
    u = x * lax.rsqrt(jnp.mean(x * x, axis=-1, keepdims=True) + EPS) * g_ref[...]
    u = u * (1.0 + sc_ref[0]) + sh_ref[0]
    p = jnp.dot(u.astype(BF16), w_ref[...], preferred_element_type=F32)
    ca, saa, sba = ca_ref[...], saa_ref[...], sba_ref[...]
    cc, sac, sbc = cc_ref[...], sac_ref[...], sbc_ref[...]

    gq = gq_ref[...]
    for h in range(A_HEADS):
        v = p[:, OFF_AQ + LANES * h:OFF_AQ + LANES * (h + 1)]
        ss = jnp.sum(v * v, axis=-1, keepdims=True) * (1.0 / HEAD_DIM)
        v = v * lax.rsqrt(ss + EPS) * gq
        qa_ref[0, h] = (_rope(v, ca, saa, sba) * qs_ab).astype(BF16)
    k = p[:, OFF_AK:OFF_AK + LANES]
    lane = lax.broadcasted_iota(jnp.int32, k.shape, 1)
    k2 = k * k
    s0 = jnp.sum(jnp.where(lane < HEAD_DIM, k2, 0.0), axis=-1, keepdims=True)
    s1 = jnp.sum(k2, axis=-1, keepdims=True) - s0
    r = jnp.where(lane < HEAD_DIM, lax.rsqrt(s0 * (1.0 / HEAD_DIM) + EPS), lax.rsqrt(s1 * (1.0 / HEAD_DIM) + EPS))
    ka_ref[0] = _rope(k * r * gk_ref[...], ca, saa, sba).astype(BF16)
    ones = jnp.ones((ONES_ROWS, k.shape[0]), BF16)
    va_ref[0] = jnp.concatenate([p[:, OFF_AV:OFF_AV + LANES].T.astype(BF16), ones], axis=0)

    for h in range(B_HEADS):
        qb_ref[0, h] = (p[:, OFF_BQ + LANES * h:OFF_BQ + LANES * (h + 1)] * qs_ab).astype(BF16)
    kb_ref[0] = p[:, OFF_BK:OFF_BK + LANES].astype(BF16)
    vb_ref[0] = jnp.concatenate([p[:, OFF_BV:OFF_BV + LANES].T.astype(BF16), ones], axis=0)

    cq = p[:, OFF_CQ:OFF_CQ + C_Q_LORA]
    cq = cq * lax.rsqrt(jnp.mean(cq * cq, axis=-1, keepdims=True) + EPS) * gcq_ref[...]
    qh = jnp.dot(cq.astype(BF16), wuq_ref[...], preferred_element_type=F32)
    for h in range(C_HEADS):
        base = 2 * LANES * h
        qc_ref[0, h, :, 0:LANES] = (qh[:, base:base + LANES] * qs_c).astype(BF16)
        qc_ref[0, h, :, LANES:2 * LANES] = (_rope(qh[:, base + LANES:base + 2 * LANES], cc, sac, sbc) * qs_c).astype(BF16)
    ckv = p[:, OFF_CKV:OFF_CKV + C_KV_LORA]
    ckv = (ckv * lax.rsqrt(jnp.mean(ckv * ckv, axis=-1, keepdims=True) + EPS) * gckv_ref[...]).astype(BF16)
    kn = jnp.dot(ckv, wuk_ref[...], preferred_element_type=F32)
    vv = jnp.dot(ckv, wuv_ref[...], preferred_element_type=F32)
    kr = _rope(p[:, OFF_CKR:OFF_CKR + LANES], cc, sac, sbc).astype(BF16)
    for pr in range(C_HEADS // 2):
        kc_ref[0, pr, :, 0:LANES] = kn[:, LANES * pr:LANES * (pr + 1)].astype(BF16)
        kc_ref[0, pr, :, LANES:2 * LANES] = kr
        vc_ref[0, pr] = jnp.concatenate([vv[:, LANES * pr:LANES * (pr + 1)].T.astype(BF16), ones], axis=0)


def _inproj(x, sc1, sh1, g, w_pad, tabs_a, tabs_c, gq, gk, gcq, gckv, wuq_pad, wuk, wuv):
    B, S, D = x.shape
    tm = _tile(S, 512)
    full = lambda shp: pl.BlockSpec(shp, lambda b, i: (0,) * len(shp))
    tab = pl.BlockSpec((tm, LANES), lambda b, i: (i, 0))
    mod = pl.BlockSpec((1, 1, D), lambda b, i: (b, 0, 0))
    kern = functools.partial(_inproj_kernel, qs_ab=HEAD_DIM ** -0.5 * LOG2E, qs_c=(C_NOPE + C_ROPE) ** -0.5 * LOG2E)
    outs = (
        jax.ShapeDtypeStruct((B, A_HEADS, S, LANES), BF16), jax.ShapeDtypeStruct((B, S, LANES), BF16),
        jax.ShapeDtypeStruct((B, VT_ROWS, S), BF16),
        jax.ShapeDtypeStruct((B, B_HEADS, S, LANES), BF16), jax.ShapeDtypeStruct((B, S, LANES), BF16),
        jax.ShapeDtypeStruct((B, VT_ROWS, S), BF16),
        jax.ShapeDtypeStruct((B, C_HEADS, S, 2 * LANES), BF16), jax.ShapeDtypeStruct((B, C_HEADS // 2, S, 2 * LANES), BF16),
        jax.ShapeDtypeStruct((B, C_HEADS // 2, VT_ROWS, S), BF16),
    )
    slab = pl.BlockSpec((1, tm, LANES), lambda b, i: (b, i, 0))
    out_specs = (
        pl.BlockSpec((1, A_HEADS, tm, LANES), lambda b, i: (b, 0, i, 0)), slab,
        pl.BlockSpec((1, VT_ROWS, tm), lambda b, i: (b, 0, i)),
        pl.BlockSpec((1, B_HEADS, tm, LANES), lambda b, i: (b, 0, i, 0)), slab,
        pl.BlockSpec((1, VT_ROWS, tm), lambda b, i: (b, 0, i)),
        pl.BlockSpec((1, C_HEADS, tm, 2 * LANES), lambda b, i: (b, 0, i, 0)),
        pl.BlockSpec((1, C_HEADS // 2, tm, 2 * LANES), lambda b, i: (b, 0, i, 0)),
        pl.BlockSpec((1, C_HEADS // 2, VT_ROWS, tm), lambda b, i: (b, 0, 0, i)),
    )
    return pl.pallas_call(
        kern, out_shape=outs, grid=(B, S // tm),
        in_specs=[pl.BlockSpec((1, tm, D), lambda b, i: (b, i, 0)), mod, mod, full((1, D)), full((D, IN_PAD)),
                  tab, tab, tab, tab, tab, tab,
                  full((1, LANES)), full((1, LANES)), full((1, C_Q_LORA)), full((1, C_KV_LORA)),
                  full(wuq_pad.shape), full(wuk.shape), full(wuv.shape)],
        out_specs=out_specs,
        compiler_params=_cparams(("arbitrary", "arbitrary")),
        name="norm_inproj",
    )(x, sc1, sh1, g, w_pad, *tabs_a, *tabs_c, gq, gk, gcq, gckv, wuq_pad, wuk, wuv)


def _flash_kernel(q_ref, k_ref, vt_ref, e_ref, o_ref, qt_ref, m_ref, acc_ref, *, tk, cw, unroll):
    H, tq, dk = q_ref.shape[2:]
    S = k_ref.shape[2]
    R = H * tq
    nc = R // cw
    nsteps = S // tk
    for h in range(H):
        qt_ref[:, h * tq:(h + 1) * tq] = q_ref[0, 0, h].astype(F32).T.astype(BF16)
    s0 = jnp.dot(k_ref[0, 0, 0:ONES_ROWS, :], qt_ref[...], preferred_element_type=F32)
    m_ref[0] = jnp.max(s0, axis=0, keepdims=True)
    acc_ref[0] = jnp.zeros(acc_ref.shape[1:], F32)

    def step(j, src, dst, lagged):
        off = pl.multiple_of(j * tk, tk)
        k = k_ref[0, 0, pl.ds(off, tk), :]
        vt = vt_ref[0, 0, :, pl.ds(off, tk)]
        scores = lambda c: jnp.dot(k, qt_ref[:, c * cw:(c + 1) * cw], preferred_element_type=F32)
        accs, ms, jumps = [], [], []
        pending = [scores(c) for c in range(min(LOOKAHEAD, nc))]
        for c in range(nc):
            sl = slice(c * cw, (c + 1) * cw)
            s = pending.pop(0)
            if c + LOOKAHEAD < nc:
                pending.append(scores(c + LOOKAHEAD))
            m_prev = m_ref[src, :, sl]
            top = jnp.max(s, axis=0, keepdims=True)
            m_new = jnp.maximum(m_prev, top)
            alpha = jnp.exp2(m_prev - m_new)
            if lagged:
                p = jnp.exp2(s - m_prev).astype(BF16)
                accs.append(alpha * (acc_ref[src, :, sl] + jnp.dot(vt, p, preferred_element_type=F32)))
                jumps.append(top - m_prev)
            else:
                p = jnp.exp2(s - m_new).astype(BF16)
                accs.append(alpha * acc_ref[src, :, sl] + jnp.dot(vt, p, preferred_element_type=F32))
            ms.append(m_new)
        acc_ref[dst] = jnp.concatenate(accs, axis=1)
        m_ref[dst] = jnp.concatenate(ms, axis=1)
        return jnp.max(jnp.concatenate(jumps, axis=1)) if lagged else None

    def guarded_step(j, src, dst):
        jump = step(j, src, dst, True)

        @pl.when(jnp.logical_not(jump <= MAX_JUMP))
        def _():
            step(j, src, dst, False)

    def body(i, carry):
        guarded_step(2 * i, 0, 1)
        guarded_step(2 * i + 1, 1, 0)
        return carry

    assert nsteps % 2 == 0 or nsteps == 1
    if nsteps == 1:
        guarded_step(0, 0, 1)
    else:
        lax.fori_loop(0, nsteps // 2, body, 0, unroll=unroll)
    last = nsteps % 2
    o_t = acc_ref[last, 0:LANES, :] / acc_ref[last, LANES:LANES + 1, :]
    out = None
    for h in range(H):
        o_h = o_t[:, h * tq:(h + 1) * tq].T.astype(BF16)
        term = jnp.dot(o_h, e_ref[h], preferred_element_type=F32)
        out = term if out is None else out + term
    o_ref[0] = out.astype(BF16)


def _flash(q, k, vt, esel, *, tq, tk, cw=256, unroll=1):
    B, P, H, S, dk = q.shape
    W = esel.shape[2]
    tq = _tile(S, tq)
    tk = _tile(S, tk)
    R = H * tq
    cw = _tile(R, cw)
    return pl.pallas_call(
        functools.partial(_flash_kernel, tk=tk, cw=cw, unroll=unroll),
        out_shape=jax.ShapeDtypeStruct((B, S, P * W), BF16),
        grid=(B, P, S // tq),
        in_specs=[pl.BlockSpec((1, 1, H, tq, dk), lambda b, p, i: (b, p, 0, i, 0)),
                  pl.BlockSpec((1, 1, S, dk), lambda b, p, i: (b, p, 0, 0)),
                  pl.BlockSpec((1, 1, VT_ROWS, S), lambda b, p, i: (b, p, 0, 0)),
                  pl.BlockSpec((H, LANES, W), lambda b, p, i: (0, 0, 0))],
        out_specs=pl.BlockSpec((1, tq, W), lambda b, p, i: (b, i, p)),
        scratch_shapes=[pltpu.VMEM((dk, R), BF16), pltpu.VMEM((2, 1, R), F32), pltpu.VMEM((2, VT_ROWS, R), F32)],
        compiler_params=_cparams(("arbitrary", "arbitrary", "arbitrary")),
        name="flash_attn",
    )(q, k, vt, esel)


def _window_kernel(q_ref, k_ref, vt_ref, e_ref, slope_ref, sink_ref, o_ref, *, tq):
    H = q_ref.shape[1]
    S = k_ref.shape[1]
    wk = min(S, tq + 2 * WINDOW)
    i = pl.program_id(1)
    start = pl.multiple_of(jnp.clip(i * tq - WINDOW, 0, S - wk), WINDOW)
    k = k_ref[0, pl.ds(start, wk), :]
    vt = vt_ref[0, :, pl.ds(start, wk)]
    spos = start + lax.broadcasted_iota(jnp.int32, (wk, tq), 0)
    tpos = i * tq + lax.broadcasted_iota(jnp.int32, (wk, tq), 1)
    dist = jnp.abs(spos - tpos)
    inside = dist <= WINDOW
    distf = dist.astype(F32)
    qt = jnp.concatenate([q_ref[0, h].astype(F32).T.astype(BF16) for h in range(H)], axis=1)
    s_all = jnp.dot(k, qt, preferred_element_type=F32)
    es, ms = [], []
    for h in range(H):
        s = s_all[:, h * tq:(h + 1) * tq] - slope_ref[h] * distf
        s = jnp.where(inside, s, NEG_INF)
        m = jnp.maximum(jnp.max(s, axis=0, keepdims=True), sink_ref[h])
        es.append(jnp.exp2(s - m).astype(BF16))
        ms.append(m)
    acc = jnp.dot(vt, jnp.concatenate(es, axis=1), preferred_element_type=F32)
    out = None
    for h in range(H):
        sl = slice(h * tq, (h + 1) * tq)
        den = acc[LANES:LANES + 1, sl] + jnp.exp2(sink_ref[h] - ms[h])
        o = (acc[0:LANES, sl] / den).T.astype(BF16)
        term = jnp.dot(o, e_ref[h], preferred_element_type=F32)
        out = term if out is None else out + term
    o_ref[0] = out.astype(BF16)


def _window(q, k, v, esel, slopes, sink, *, tq):
    B, H, S, _ = q.shape
    W = esel.shape[2]
    tq = _tile(S, tq)
    return pl.pallas_call(
        functools.partial(_window_kernel, tq=tq),
        out_shape=jax.ShapeDtypeStruct((B, S, W), BF16),
        grid=(B, S // tq),
        in_specs=[pl.BlockSpec((1, H, tq, LANES), lambda b, i: (b, 0, i, 0)),
                  pl.BlockSpec((1, S, LANES), lambda b, i: (b, 0, 0)),
                  pl.BlockSpec((1, VT_ROWS, S), lambda b, i: (b, 0, 0)),
                  pl.BlockSpec((H, LANES, W), lambda b, i: (0, 0, 0)),
                  pl.BlockSpec((H, 1, 1), lambda b, i: (0, 0, 0)),
                  pl.BlockSpec((H, 1, 1), lambda b, i: (0, 0, 0))],
        out_specs=pl.BlockSpec((1, tq, W), lambda b, i: (b, i, 0)),
        compiler_params=_cparams(("arbitrary", "arbitrary")),
        name="window_attn",
    )(q, k, v, esel, slopes, sink)


def _sigmoid(v):
    return 1.0 / (1.0 + jnp.exp(-v))


def _pack_token_words(x):
    rp = x.shape[1] // (2 * LANES)
    words = []
    for r in range(rp):
        lo = lax.bitcast_convert_type(x[:, LANES * r:LANES * (r + 1)].astype(BF16).astype(F32), jnp.uint32)
        hi = lax.bitcast_convert_type(x[:, LANES * (rp + r):LANES * (rp + r + 1)].astype(BF16).astype(F32), jnp.uint32)
        words.append((lo >> 16) | hi)
    return words


def _load_token_words(ref, row0, n, rp):
    return [ref[pl.ds(row0 + r, n, stride=rp), :] for r in range(rp)]


def _unpack_token_words(words):
    lo = [lax.bitcast_convert_type(w << 16, F32) for w in words]
    hi = [lax.bitcast_convert_type(w & jnp.uint32(0xFFFF0000), F32) for w in words]
    return jnp.concatenate(lo + hi, axis=1)


def _post_kernel(oa_ref, ob_ref, oc_ref, x_ref, g1_ref, sc2_ref, sh2_ref, g2_ref, og_ref, wout_ref, n2_ref, rwt_ref,
                 sw1_ref, sw3_ref, sw2_ref, xs1_ref, u2_ref, st_ref):
    og = og_ref[...]

    def rn(o_ref, g):
        o = o_ref[0].astype(F32)
        return (o * lax.rsqrt(jnp.mean(o * o, axis=-1, keepdims=True) + EPS) * g).astype(BF16)

    mixed = jnp.concatenate([rn(oa_ref, og[:, :A_WIDTH]), rn(ob_ref, og[:, A_WIDTH:A_WIDTH + B_WIDTH]),
                             rn(oc_ref, og[:, A_WIDTH + B_WIDTH:])], axis=-1)
    x1 = x_ref[0] + g1_ref[0] * jnp.dot(mixed, wout_ref[...], preferred_element_type=F32)
    u2 = x1 * lax.rsqrt(jnp.mean(x1 * x1, axis=-1, keepdims=True) + EPS) * n2_ref[...]
    u2 = u2 * (1.0 + sc2_ref[0]) + sh2_ref[0]
    words = _pack_token_words(u2)
    for r, w in enumerate(words):
        u2_ref[pl.ds(r, u2.shape[0], stride=len(words)), :] = w
    ub = u2.astype(BF16)
    logits_t = lax.dot_general(rwt_ref[...], ub, (((1,), (1,)), ((), ())), preferred_element_type=F32)
    st_ref[...] = _sigmoid(logits_t)
    h1 = jnp.dot(ub, sw1_ref[...], preferred_element_type=F32)
    h3 = jnp.dot(ub, sw3_ref[...], preferred_element_type=F32)
    hid = (h1 * _sigmoid(h1) * h3).astype(BF16)
    xs1_ref[0] = x1 + g2_ref[0] * jnp.dot(hid, sw2_ref[...], preferred_element_type=F32)


def _post(oa, ob, oc, x, g1, sc2, sh2, g2, og, wout, n2, rwt, sw1, sw3, sw2):
    B, S, D = x.shape
    E = rwt.shape[0]
    tm = _tile(S, 512)
    nt = S // tm
    full = lambda a: pl.BlockSpec(a.shape, lambda b, i: (0,) * a.ndim)
    mod = pl.BlockSpec((1, 1, D), lambda b, i: (b, 0, 0))
    row = lambda w: pl.BlockSpec((1, tm, w), lambda b, i: (b, i, 0))
    return pl.pallas_call(
        _post_kernel,
        out_shape=(jax.ShapeDtypeStruct((B, S, D), F32),
                   jax.ShapeDtypeStruct((B * S * D // (2 * LANES), LANES), jnp.uint32),
                   jax.ShapeDtypeStruct((E, B * S), F32)),
        grid=(B, nt),
        in_specs=[row(A_WIDTH), row(B_WIDTH), row(C_WIDTH), row(D), mod, mod, mod, mod, full(og), full(wout), full(n2),
                  full(rwt), full(sw1), full(sw3), full(sw2)],
        out_specs=(row(D), pl.BlockSpec((tm * D // (2 * LANES), LANES), lambda b, i: (b * nt + i, 0)),
                   pl.BlockSpec((E, tm), lambda b, i: (0, b * nt + i))),
        compiler_params=_cparams(("arbitrary", "arbitrary")),
        name="post_attn",
    )(oa, ob, oc, x, g1, sc2, sh2, g2, og, wout, n2, rwt, sw1, sw3, sw2)


def _route_kernel(st_ref, rb_ref, tri_ref, idx_ref, w_ref, rank_ref, cnt_ref, carry_ref):
    E, tm = st_ref.shape
    per = E // N_GROUPS
    big = float(E)

    @pl.when(pl.program_id(0) == 0)
    def _():
        carry_ref[...] = jnp.zeros(carry_ref.shape, F32)

    s = st_ref[...]
    choice = s + rb_ref[...]
    c3 = choice.reshape(N_GROUPS, per, tm)
    io3 = lax.broadcasted_iota(jnp.int32, c3.shape, 1).astype(F32)
    m1 = jnp.max(c3, axis=1, keepdims=True)
    i1 = jnp.min(jnp.where(c3 == m1, io3, big), axis=1, keepdims=True)
    m2 = jnp.max(jnp.where(io3 == i1, -jnp.inf, c3), axis=1, keepdims=True)
    gs = (m1 + m2).reshape(N_GROUPS, tm)
    gio = lax.broadcasted_iota(jnp.int32, gs.shape, 0).astype(F32)
    keep = jnp.zeros(gs.shape, F32)
    for _ in range(TOPK_GROUPS):
        mg = jnp.max(gs, axis=0, keepdims=True)
        ig = jnp.min(jnp.where(gs == mg, gio, big), axis=0, keepdims=True)
        sel = gio == ig
        keep = jnp.where(sel, 1.0, keep)
        gs = jnp.where(sel, -jnp.inf, gs)
    cur = jnp.where(keep.reshape(N_GROUPS, 1, tm) > 0.5, c3, NEG_INF).reshape(E, tm)
    eio = lax.broadcasted_iota(jnp.int32, (E, tm), 0).astype(F32)
    sels, idxs, ws = [], [], []
    for _ in range(TOP_K):
        m = jnp.max(cur, axis=0, keepdims=True)
        ik = jnp.min(jnp.where(cur == m, eio, big), axis=0, keepdims=True)
        sel = eio == ik
        ws.append(jnp.sum(jnp.where(sel, s, 0.0), axis=0, keepdims=True))
        cur = jnp.where(sel, -jnp.inf, cur)
        sels.append(sel)
        idxs.append(ik)
    wsum = ws[0]
    for wk in ws[1:]:
        wsum = wsum + wk
    w_ref[...] = jnp.concatenate([wk / wsum * ROUTED_SCALE for wk in ws], axis=0)
    idx_ref[...] = jnp.concatenate(idxs, axis=0).astype(jnp.int32)
    onehot = jnp.zeros((E, tm), F32)
    for sel in sels:
        onehot = jnp.where(sel, 1.0, onehot)
    prefix = jnp.dot(onehot.astype(BF16), tri_ref[...], preferred_element_type=F32) + carry_ref[...]
    rank_ref[...] = jnp.concatenate(
        [jnp.sum(jnp.where(sel, prefix, 0.0), axis=0, keepdims=True) for sel in sels], axis=0).astype(jnp.int32)
    carry = carry_ref[...] + jnp.sum(onehot, axis=1, keepdims=True)
    carry_ref[...] = carry
    cnt_ref[...] = jnp.broadcast_to(carry, cnt_ref.shape)


def _route(scores_t, router_b):
    E, N = scores_t.shape
    tm = _tile(N, 512)
    tri = (jnp.arange(tm)[:, None] < jnp.arange(tm)[None, :]).astype(BF16)
    return pl.pallas_call(
        _route_kernel,
        out_shape=(jax.ShapeDtypeStruct((TOP_K, N), jnp.int32), jax.ShapeDtypeStruct((TOP_K, N), F32),
                   jax.ShapeDtypeStruct((TOP_K, N), jnp.int32), jax.ShapeDtypeStruct((E, LANES), F32)),
        grid=(N // tm,),
        in_specs=[pl.BlockSpec((E, tm), lambda i: (0, i)), pl.BlockSpec((E, 1), lambda i: (0, 0)),
                  pl.BlockSpec((tm, tm), lambda i: (0, 0))],
        out_specs=(pl.BlockSpec((TOP_K, tm), lambda i: (0, i)), pl.BlockSpec((TOP_K, tm), lambda i: (0, i)),
                   pl.BlockSpec((TOP_K, tm), lambda i: (0, i)), pl.BlockSpec((E, LANES), lambda i: (0, 0))),
        scratch_shapes=[pltpu.VMEM((E, 1), F32)],
        compiler_params=_cparams(("arbitrary",)),
        name="route_topk",
    )(scores_t, router_b.reshape(E, 1), tri)


def _slots_kernel(idx_ref, rank_ref, starts_ref, dest_ref):
    E = starts_ref.shape[0]
    tm = idx_ref.shape[1]
    eio = lax.broadcasted_iota(jnp.int32, (E, tm), 0)
    starts = starts_ref[...]
    rows = [jnp.sum(jnp.where(eio == idx_ref[k:k + 1, :], starts, 0.0), axis=0, keepdims=True) for k in range(TOP_K)]
    dest_ref[...] = jnp.concatenate(rows, axis=0).astype(jnp.int32) + rank_ref[...]


def _slots(idx, rank, starts):
    K, N = idx.shape
    E = starts.shape[0]
    tm = _tile(N, 512)
    blk = pl.BlockSpec((K, tm), lambda i: (0, i))
    return pl.pallas_call(
        _slots_kernel, out_shape=jax.ShapeDtypeStruct((K, N), jnp.int32), grid=(N // tm,),
        in_specs=[blk, blk, pl.BlockSpec((E, 1), lambda i: (0, 0))], out_specs=blk,
        compiler_params=_cparams(("arbitrary",)), name="moe_slots",
    )(idx, rank, starts.astype(F32).reshape(E, 1))


TOKENS_PER_ISSUE = 8


def _dispatch_kernel(dest_ref, u_ref, xs_ref, sem, *, ch):
    tm = u_ref.shape[0] // ch

    def body(g, carry):
        base = pl.multiple_of(g * (TOKENS_PER_ISSUE * ch), TOKENS_PER_ISSUE * ch)
        for r in range(TOKENS_PER_ISSUE):
            for k in range(TOP_K):
                d = dest_ref[g * (TOKENS_PER_ISSUE * TOP_K) + r * TOP_K + k]
                pltpu.make_async_copy(u_ref.at[pl.ds(base + r * ch, ch)],
                                      xs_ref.at[pl.ds(pl.multiple_of(d * ch, ch), ch)],
                                      sem).start(priority=k % 2)
        return carry

    lax.fori_loop(0, tm // TOKENS_PER_ISSUE, body, 0)
    for k in range(TOP_K):
        pltpu.make_async_copy(u_ref, xs_ref.at[pl.ds(0, tm * ch)], sem).wait()


def _dispatch(dest_flat, u2t, n_slots, ch):
    N = u2t.shape[0] // ch
    tm = _tile(N, 256)
    return pl.pallas_call(
        functools.partial(_dispatch_kernel, ch=ch),
        out_shape=jax.ShapeDtypeStruct((n_slots * ch, LANES), u2t.dtype),
        grid=(N // tm,),
        in_specs=[pl.BlockSpec((tm * TOP_K,), lambda i: (i,), memory_space=pltpu.SMEM),
                  pl.BlockSpec((tm * ch, LANES), lambda i: (i, 0))],
        out_specs=pl.BlockSpec(memory_space=pl.ANY),
        scratch_shapes=[pltpu.SemaphoreType.DMA(())],
        compiler_params=pltpu.CompilerParams(dimension_semantics=("arbitrary",), vmem_limit_bytes=VMEM_LIMIT,
                                             has_side_effects=True),
        name="moe_dispatch",
    )(dest_flat, u2t)


def _expert_kernel(wb_ref, we_ref, lo_ref, hi_ref, nw_ref, xs_ref, w1_ref, w3_ref, w2_ref, ys_ref, w1b, w3b, w2b):
    j = pl.program_id(0)

    @pl.when(j < nw_ref[0])
    def _():
        prev = jnp.maximum(j - 1, 0)
        new_expert = jnp.logical_or(j == 0, we_ref[j] != we_ref[prev])
        new_block = jnp.logical_or(j == 0, wb_ref[j] != wb_ref[prev])

        @pl.when(new_expert)
        def _():
            w1b[...] = w1_ref[0].astype(BF16)
            w3b[...] = w3_ref[0].astype(BF16)
            w2b[...] = w2_ref[0].astype(BF16)

        rp = w1b.shape[0] // (2 * LANES)
        rows = lax.broadcasted_iota(jnp.int32, (MOE_BM, LANES), 0)
        mine = (rows >= lo_ref[j]) & (rows < hi_ref[j])
        words = [[] for _ in range(rp)]
        for s in range(MOE_SPLIT):
            r0, nr = s * (MOE_BM // MOE_SPLIT), MOE_BM // MOE_SPLIT
            xw = [jnp.where(mine[r0:r0 + nr], w, jnp.uint32(0)) for w in _load_token_words(xs_ref, r0 * rp, nr, rp)]
            xb = _unpack_token_words(xw).astype(BF16)
            h1 = jnp.dot(xb, w1b[...], preferred_element_type=F32)
            h3 = jnp.dot(xb, w3b[...], preferred_element_type=F32)
            hid = (h1 * _sigmoid(h1) * h3).astype(BF16)
            for r, w in enumerate(_pack_token_words(jnp.dot(hid, w2b[...], preferred_element_type=F32))):
                words[r].append(w)
        words = [jnp.concatenate(ws, axis=0) for ws in words]

        @pl.when(new_block)
        def _():
            for r, w in enumerate(words):
                ys_ref[pl.ds(r, MOE_BM, stride=rp), :] = w

        @pl.when(jnp.logical_not(new_block))
        def _():
            for r, prev in enumerate(_load_token_words(ys_ref, 0, MOE_BM, rp)):
                ys_ref[pl.ds(r, MOE_BM, stride=rp), :] = jnp.where(mine, words[r], prev)


def _experts(wb, we, lo, hi, nw, xs, w1, w3, w2, *, layer):
    D, F = w1.shape[2:]
    rows = MOE_BM * D // (2 * LANES)
    blk = lambda j, wb, we, lo, hi, nw: (wb[j], 0)
    wsel = lambda j, wb, we, lo, hi, nw: (layer, we[j], 0, 0)
    grid_spec = pltpu.PrefetchScalarGridSpec(
        num_scalar_prefetch=5, grid=(wb.shape[0],),
        in_specs=[pl.BlockSpec((rows, LANES), blk), pl.BlockSpec((None, 1, D, F), wsel),
                  pl.BlockSpec((None, 1, D, F), wsel), pl.BlockSpec((None, 1, F, D), wsel)],
        out_specs=pl.BlockSpec((rows, LANES), blk),
        scratch_shapes=[pltpu.VMEM((D, F), BF16), pltpu.VMEM((D, F), BF16), pltpu.VMEM((F, D), BF16)])
    return pl.pallas_call(
        _expert_kernel, out_shape=jax.ShapeDtypeStruct(xs.shape, xs.dtype), grid_spec=grid_spec,
        compiler_params=_cparams(("arbitrary",)), name="moe_experts",
    )(wb, we, lo, hi, nw, xs, w1, w3, w2)


def _combine_kernel(dcur_ref, dnext_ref, w_ref, xs1_ref, g2_ref, fg_ref, ys_ref, o_ref, buf, sems, *, final):
    tm, D = xs1_ref.shape
    ch = D // (2 * LANES)
    i = pl.program_id(0)
    n = pl.num_programs(0)

    def issue(dref, slot):
        def body(g, carry):
            base = pl.multiple_of(g * (TOKENS_PER_ISSUE * ch), TOKENS_PER_ISSUE * ch)
            for r in range(TOKENS_PER_ISSUE):
                for k in range(TOP_K):
                    d = dref[g * (TOKENS_PER_ISSUE * TOP_K) + r * TOP_K + k]
                    pltpu.make_async_copy(ys_ref.at[pl.ds(pl.multiple_of(d * ch, ch), ch)],
                                          buf.at[slot, k, pl.ds(base + r * ch, ch)],
                                          sems.at[slot]).start(priority=k % 2)
            return carry

        lax.fori_loop(0, tm // TOKENS_PER_ISSUE, body, 0)

    def finish(slot):
        for k in range(TOP_K):
            pltpu.make_async_copy(ys_ref.at[pl.ds(0, tm * ch)], buf.at[slot, k], sems.at[slot]).wait()
        w = w_ref[...]
        r = None
        for k in range(TOP_K):
            term = w[:, k:k + 1] * _unpack_token_words(_load_token_words(buf.at[slot, k], 0, tm, ch))
            r = term if r is None else r + term
        x2 = xs1_ref[...] + g2_ref[0] * r
        if final:
            x2 = x2 * lax.rsqrt(jnp.mean(x2 * x2, axis=-1, keepdims=True) + EPS) * fg_ref[...]
        o_ref[...] = x2

    @pl.when(i == 0)
    def _():
        issue(dcur_ref, 0)

    for slot in range(2):
        @pl.when(jnp.logical_and(i + 1 < n, (i + 1) % 2 == slot))
        def _():
            issue(dnext_ref, slot)

    for slot in range(2):
        @pl.when(i % 2 == slot)
        def _():
            finish(slot)


def _combine(dest_flat, w_tok, xs1, g2, fg, ys, *, seq, final):
    N, D = xs1.shape
    ch = D // (2 * LANES)
    tm = _tile(seq, 128)
    per_b = seq // tm
    n = N // tm
    return pl.pallas_call(
        functools.partial(_combine_kernel, final=final),
        out_shape=jax.ShapeDtypeStruct((N, D), F32),
        grid=(n,),
        in_specs=[pl.BlockSpec((tm * TOP_K,), lambda i: (i,), memory_space=pltpu.SMEM),
                  pl.BlockSpec((tm * TOP_K,), lambda i: (jnp.minimum(i + 1, n - 1),), memory_space=pltpu.SMEM),
                  pl.BlockSpec((tm, TOP_K), lambda i: (i, 0)),
                  pl.BlockSpec((tm, D), lambda i: (i, 0)),
                  pl.BlockSpec((1, 1, D), lambda i: (i // per_b, 0, 0)),
                  pl.BlockSpec((1, D), lambda i: (0, 0)),
                  pl.BlockSpec(memory_space=pl.ANY)],
        out_specs=pl.BlockSpec((tm, D), lambda i: (i, 0)),
        scratch_shapes=[pltpu.VMEM((2, TOP_K, tm * ch, LANES), jnp.uint32), pltpu.SemaphoreType.DMA((2,))],
        compiler_params=_cparams(("arbitrary",)),
        name="moe_combine",
    )(dest_flat, dest_flat, w_tok, xs1, g2, fg, ys)


def _rope_tables(S):
    half = HEAD_DIM // 2
    inv = ROPE_THETA ** (-jnp.arange(0, half, 2, dtype=F32) / half)
    t = jnp.arange(S, dtype=jnp.int32)

    def part(pos):
        ang = pos.astype(F32)[:, None] * inv[None, :]
        c, s, z = jnp.cos(ang), jnp.sin(ang), jnp.zeros_like(ang)
        return jnp.concatenate([c, c], -1), jnp.concatenate([-s, z], -1), jnp.concatenate([z, s], -1)

    row, col, lin = part(t // GRID_W), part(t % GRID_W), part(t)
    tabs_a = tuple(jnp.tile(jnp.concatenate([r, c], -1), (1, LANES // HEAD_DIM)) for r, c in zip(row, col))
    tabs_c = tuple(jnp.pad(v, ((0, 0), (0, LANES - C_ROPE))) for v in lin)
    return tabs_a, tabs_c


def _head_slabs(w, heads, groups):
    d = w.shape[0]
    hot = (jnp.arange(heads)[:, None] // (heads // groups) == jnp.arange(groups)[None, :]).astype(w.dtype)
    return (w.reshape(d, heads, 1, HEAD_DIM) * hot[None, :, :, None]).reshape(d, heads * LANES)


def _pad_w_in(w):
    d = w.shape[0]
    a, b, c = w[:, :A_IN], w[:, A_IN:A_IN + B_IN], w[:, A_IN + B_IN:]
    kr = jnp.pad(c[:, C_Q_LORA + C_KV_LORA:], ((0, 0), (0, LANES - C_ROPE)))
    return jnp.concatenate([_head_slabs(a[:, :A_WIDTH], A_HEADS, A_KV_HEADS), a[:, A_WIDTH:],
                            _head_slabs(b[:, :B_WIDTH], B_HEADS, B_KV_HEADS), b[:, B_WIDTH:],
                            c[:, :C_Q_LORA + C_KV_LORA], kr], axis=1).astype(BF16)


def _pad_w_uq(w):
    r = w.shape[0]
    w = w.reshape(r, C_HEADS, C_NOPE + C_ROPE)
    hot = (jnp.arange(C_HEADS)[:, None] % 2 == jnp.arange(2)[None, :]).astype(w.dtype)
    nope = (w[:, :, None, :C_NOPE] * hot[None, :, :, None]).reshape(r, C_HEADS, LANES)
    rope = jnp.pad(w[:, :, C_NOPE:], ((0, 0), (0, 0), (0, LANES - C_ROPE)))
    return jnp.concatenate([nope, rope], axis=-1).reshape(r, C_HEADS * 2 * LANES).astype(BF16)


def _select_mats(heads, groups, out_w):
    h = jnp.arange(heads)[:, None, None]
    r = jnp.arange(LANES)[None, :, None]
    c = jnp.arange(out_w)[None, None, :]
    half = h // (heads // groups)
    return ((r // HEAD_DIM == half) & (c == HEAD_DIM * h + r % HEAD_DIM)).astype(BF16)


def kernel(x, c, mod_w, mod_b, norm1_g, norm2_g, w_in, a_qnorm_g, a_knorm_g, b_sink, c_qnorm_g, c_kvnorm_g, c_w_uq, c_w_uk, c_w_uv, out_norm_g, w_out, router_w, router_b, exp_w1, exp_w3, exp_w2, sh_w1, sh_w3, sh_w2, final_g):
    B, S, D = x.shape
    L = mod_w.shape[0]
    E = router_w.shape[2]
    N = B * S
    n_slots = N * TOP_K
    assert n_slots % MOE_BM == 0
    n_work = n_slots // MOE_BM + E

    mod = _mod(c, mod_w, mod_b).reshape(L, B, 6, 1, D)
    tabs_a, tabs_c = _rope_tables(S)
    esel_a = _select_mats(A_HEADS, A_KV_HEADS, A_WIDTH)
    esel_c = _select_mats(2, 2, LANES)
    slopes = (jnp.exp2(-8.0 * jnp.arange(1, B_HEADS + 1, dtype=F32) / B_HEADS) * LOG2E).reshape(B_HEADS, 1, 1)

    for l in range(L):
        sh1, sc1, g1, sh2, sc2, g2 = [mod[l, :, i] for i in range(6)]
        gq2 = jnp.tile(a_qnorm_g[l], LANES // HEAD_DIM).reshape(1, LANES)
        gk2 = jnp.tile(a_knorm_g[l], LANES // HEAD_DIM).reshape(1, LANES)
        qa, ka, va, qb, kb, vb, qc, kc, vc = _inproj(
            x, sc1, sh1, norm1_g[l].reshape(1, D), _pad_w_in(w_in[l]), tabs_a, tabs_c, gq2, gk2,
            c_qnorm_g[l].reshape(1, -1), c_kvnorm_g[l].reshape(1, -1), _pad_w_uq(c_w_uq[l]),
            c_w_uk[l].astype(BF16), c_w_uv[l].astype(BF16))
        oa = _flash(qa[:, None], ka[:, None], va[:, None], esel_a, tq=1024, tk=512, cw=512, unroll=1)
        ob = _window(qb, kb, vb, esel_a, slopes, (b_sink[l].astype(F32) * LOG2E).reshape(B_HEADS, 1, 1), tq=256)
        oc = _flash(qc.reshape(B, C_HEADS // 2, 2, S, 2 * LANES), kc, vc, esel_c, tq=2048, tk=512, cw=512, unroll=1)
        xs1, u2, scores_t = _post(
            oa, ob, oc, x, g1, sc2, sh2, g2, out_norm_g[l].reshape(1, -1), w_out[l].astype(BF16),
            norm2_g[l].reshape(1, D), router_w[l].T.astype(BF16), sh_w1[l].astype(BF16), sh_w3[l].astype(BF16),
            sh_w2[l].astype(BF16))
        idx, w_top, rank, cnt = _route(scores_t, router_b[l])
        counts = cnt[:, 0].astype(jnp.int32)
        ends = jnp.cumsum(counts)
        starts = ends - counts
        dest = _slots(idx, rank, starts).T.reshape(N * TOP_K)
        first_blk = starts // MOE_BM
        n_items = jnp.where(counts > 0, (ends - 1) // MOE_BM - first_blk + 1, 0)
        item_ends = jnp.cumsum(n_items)
        nw = item_ends[-1].reshape(1).astype(jnp.int32)
        wid = jnp.minimum(jnp.arange(n_work, dtype=jnp.int32), nw[0] - 1)
        we = jnp.minimum(jnp.sum(item_ends[None, :] <= wid[:, None], axis=1), E - 1).astype(jnp.int32)
        hot = we[:, None] == jnp.arange(E, dtype=jnp.int32)[None, :]
        take = lambda tbl: jnp.sum(jnp.where(hot, tbl[None, :], 0), axis=1).astype(jnp.int32)
        wb = take(first_blk) + wid - take(item_ends - n_items)
        lo = jnp.clip(take(starts) - wb * MOE_BM, 0, MOE_BM).astype(jnp.int32)
        hi = jnp.clip(take(ends) - wb * MOE_BM, 0, MOE_BM).astype(jnp.int32)
        xs = _dispatch(dest, u2, n_slots, D // (2 * LANES))
        ys = _experts(wb, we, lo, hi, nw, xs, exp_w1, exp_w3, exp_w2, layer=l)
        x = _combine(dest, w_top.T, xs1.reshape(N, D), g2, final_g.reshape(1, D), ys, seq=S,
                     final=(l == L - 1)).reshape(B, S, D)
    return x
```

```python
import functools

import jax
import jax.numpy as jnp
from jax import lax
from jax.experimental import pallas as pl
from jax.experimental.pallas import tpu as pltpu

F32 = jnp.float32
BF16 = jnp.bfloat16

HEAD_DIM = 64
GRID_W = 64
ROPE_THETA = 10000.0
EPS = 1e-6
NEG_INF = -1e30
A_HEADS, A_KV_HEADS = 6, 2
B_HEADS, B_KV_HEADS = 6, 2
WINDOW = 128
C_HEADS, C_Q_LORA, C_KV_LORA, C_NOPE, C_ROPE, C_V = 4, 256, 128, 64, 32, 64
A_WIDTH = A_HEADS * HEAD_DIM
B_WIDTH = B_HEADS * HEAD_DIM
C_WIDTH = C_HEADS * C_V
A_IN = (A_HEADS + 2 * A_KV_HEADS) * HEAD_DIM
B_IN = (B_HEADS + 2 * B_KV_HEADS) * HEAD_DIM
TOP_K = 8
N_GROUPS = 8
TOPK_GROUPS = 4
ROUTED_SCALE = 2.5
LOG2E = 1.4426950408889634

LANES = 128
SUBLANE_BITS = 3
SUBLANES = 1 << SUBLANE_BITS
OFF_AQ = 0
OFF_AK = OFF_AQ + A_HEADS * LANES
OFF_AV = OFF_AK + LANES
OFF_BQ = OFF_AV + LANES
OFF_BK = OFF_BQ + B_HEADS * LANES
OFF_BV = OFF_BK + LANES
OFF_CQ = OFF_BV + LANES
OFF_CKV = OFF_CQ + C_Q_LORA
OFF_CKR = OFF_CKV + C_KV_LORA
IN_PAD = OFF_CKR + LANES

ONES_ROWS = 16
VT_ROWS = LANES + ONES_ROWS
LOOKAHEAD = 3
MAX_JUMP = 64.0
MOE_BM = 512
MOE_SPLIT = 2
VMEM_LIMIT = 56 * 1024 * 1024


def _tile(n, pref):
    t = min(n, pref)
    assert n % t == 0, (n, t)
    return t


def _cparams(sem, vmem=VMEM_LIMIT):
    return pltpu.CompilerParams(dimension_semantics=sem, vmem_limit_bytes=vmem)


def _mod_kernel(c_ref, w_ref, b_ref, o_ref):
    c = c_ref[...]
    ca = c * (1.0 / (1.0 + jnp.exp(-c)))
    o_ref[0] = jnp.dot(ca, w_ref[0], preferred_element_type=F32, precision=lax.Precision.HIGHEST) + b_ref[0]


def _mod(c, mod_w, mod_b):
    L, D, W = mod_w.shape
    B = c.shape[0]
    tn = _tile(W, 1536)
    return pl.pallas_call(
        _mod_kernel,
        out_shape=jax.ShapeDtypeStruct((L, B, W), F32),
        grid=(L, W // tn),
        in_specs=[
            pl.BlockSpec((B, D), lambda l, j: (0, 0)),
            pl.BlockSpec((1, D, tn), lambda l, j: (l, 0, j)),
            pl.BlockSpec((1, 1, tn), lambda l, j: (l, 0, j)),
        ],
        out_specs=pl.BlockSpec((1, B, tn), lambda l, j: (l, 0, j)),
        compiler_params=_cparams(("arbitrary", "arbitrary")),
        name="mod_proj",
    )(c, mod_w, mod_b.reshape(L, 1, W))


def _rope(v, c, sa, sb):
    return v * c + pltpu.roll(v, LANES - 16, 1) * sa + pltpu.roll(v, 16, 1) * sb


def _inproj_kernel(x_ref, sc_ref, sh_ref, g_ref, w_ref, ca_ref, saa_ref, sba_ref, cc_ref, sac_ref, sbc_ref,
                   gq_ref, gk_ref, gcq_ref, gckv_ref, wuq_ref, wuk_ref, wuv_ref,
                   qa_ref, ka_ref, va_ref, qb_ref, kb_ref, vb_ref, qc_ref, kc_ref, vc_ref, *, qs_ab, qs_c):
    tm = x_ref.shape[1]
    nh = 2 if tm % (2 * LANES) == 0 else 1
    hm = tm // nh

    def part(r0):
        rows = slice(r0, r0 + hm)
        x = x_ref[0, rows, :]
        u = x * lax.rsqrt(jnp.mean(x * x, axis=-1, keepdims=True) + EPS) * g_ref[...]
        u = u * (1.0 + sc_ref[0]) + sh_ref[0]
        p = jnp.dot(u.astype(BF16), w_ref[...], preferred_element_type=F32)
        ca, saa, sba = ca_ref[rows, :], saa_ref[rows, :], sba_ref[rows, :]
        cc, sac, sbc = cc_ref[rows, :], sac_ref[rows, :], sbc_ref[rows, :]
        out = {}
        gq = gq_ref[...]
        qa = []
        for h in range(A_HEADS):
            v = p[:, OFF_AQ + LANES * h:OFF_AQ + LANES * (h + 1)]
            ss = jnp.sum(v * v, axis=-1, keepdims=True) * (1.0 / HEAD_DIM)
            v = v * lax.rsqrt(ss + EPS) * gq
            qa.append((_rope(v, ca, saa, sba) * qs_ab).astype(BF16))
        out["qa"] = qa
        k = p[:, OFF_AK:OFF_AK + LANES]
        lane = lax.broadcasted_iota(jnp.int32, k.shape, 1)
        k2 = k * k
        s0 = jnp.sum(jnp.where(lane < HEAD_DIM, k2, 0.0), axis=-1, keepdims=True)
        s1 = jnp.sum(k2, axis=-1, keepdims=True) - s0
        r = jnp.where(lane < HEAD_DIM, lax.rsqrt(s0 * (1.0 / HEAD_DIM) + EPS), lax.rsqrt(s1 * (1.0 / HEAD_DIM) + EPS))
        out["ka"] = _rope(k * r * gk_ref[...], ca, saa, sba).astype(BF16)
        out["va"] = p[:, OFF_AV:OFF_AV + LANES].T.astype(BF16)
        out["qb"] = [(p[:, OFF_BQ + LANES * h:OFF_BQ + LANES * (h + 1)] * qs_ab).astype(BF16) for h in range(B_HEADS)]
        out["kb"] = p[:, OFF_BK:OFF_BK + LANES].astype(BF16)
        out["vb"] = p[:, OFF_BV:OFF_BV + LANES].T.astype(BF16)
        cq = p[:, OFF_CQ:OFF_CQ + C_Q_LORA]
        cq = cq * lax.rsqrt(jnp.mean(cq * cq, axis=-1, keepdims=True) + EPS) * gcq_ref[...]
        qh = jnp.dot(cq.astype(BF16), wuq_ref[...], preferred_element_type=F32)
        qc = []
        for h in range(C_HEADS):
            base = 2 * LANES * h
            qc.append(jnp.concatenate(
                [(qh[:, base:base + LANES] * qs_c).astype(BF16),
                 (_rope(qh[:, base + LANES:base + 2 * LANES], cc, sac, sbc) * qs_c).astype(BF16)], axis=1))
        out["qc"] = qc
        ckv = p[:, OFF_CKV:OFF_CKV + C_KV_LORA]
        ckv = (ckv * lax.rsqrt(jnp.mean(ckv * ckv, axis=-1, keepdims=True) + EPS) * gckv_ref[...]).astype(BF16)
        kn = jnp.dot(ckv, wuk_ref[...], preferred_element_type=F32)
        vv = jnp.dot(ckv, wuv_ref[...], preferred_element_type=F32)
        kr = _rope(p[:, OFF_CKR:OFF_CKR + LANES], cc, sac, sbc).astype(BF16)
        out["kc"] = [jnp.concatenate([kn[:, LANES * pr:LANES * (pr + 1)].astype(BF16), kr], axis=1)
                     for pr in range(C_HEADS // 2)]
        out["vc"] = [vv[:, LANES * pr:LANES * (pr + 1)].T.astype(BF16) for pr in range(C_HEADS // 2)]
        return out

    parts = [part(i * hm) for i in range(nh)]
    rowcat = lambda f: jnp.concatenate([f(o) for o in parts], axis=0)
    ones = jnp.ones((ONES_ROWS, tm), BF16)
    colcat = lambda f: jnp.concatenate([jnp.concatenate([f(o) for o in parts], axis=1), ones], axis=0)
    for h in range(A_HEADS):
        qa_ref[0, h] = rowcat(lambda o: o["qa"][h])
    ka_ref[0] = rowcat(lambda o: o["ka"])
    va_ref[0] = colcat(lambda o: o["va"])
    for h in range(B_HEADS):
        qb_ref[0, h] = rowcat(lambda o: o["qb"][h])
    kb_ref[0] = rowcat(lambda o: o["kb"])
    vb_ref[0] = colcat(lambda o: o["vb"])
    for h in range(C_HEADS):
        qc_ref[0, h] = rowcat(lambda o: o["qc"][h])
    for pr in range(C_HEADS // 2):
        kc_ref[0, pr] = rowcat(lambda o: o["kc"][pr])
        vc_ref[0, pr] = colcat(lambda o: o["vc"][pr])


def _inproj(x, sc1, sh1, g, w_pad, tabs_a, tabs_c, gq, gk, gcq, gckv, wuq_pad, wuk, wuv):
    B, S, D = x.shape
    tm = _tile(S, 512)
    full = lambda shp: pl.BlockSpec(shp, lambda b, i: (0,) * len(shp))
    tab = pl.BlockSpec((tm, LANES), lambda b, i: (i, 0))
    mod = pl.BlockSpec((1, 1, D), lambda b, i: (b, 0, 0))
    kern = functools.partial(_inproj_kernel, qs_ab=HEAD_DIM ** -0.5 * LOG2E, qs_c=(C_NOPE + C_ROPE) ** -0.5 * LOG2E)
    outs = (
        jax.ShapeDtypeStruct((B, A_HEADS, S, LANES), BF16), jax.ShapeDtypeStruct((B, S, LANES), BF16),
        jax.ShapeDtypeStruct((B, VT_ROWS, S), BF16),
        jax.ShapeDtypeStruct((B, B_HEADS, S, LANES), BF16), jax.ShapeDtypeStruct((B, S, LANES), BF16),
        jax.ShapeDtypeStruct((B, VT_ROWS, S), BF16),
        jax.ShapeDtypeStruct((B, C_HEADS, S, 2 * LANES), BF16), jax.ShapeDtypeStruct((B, C_HEADS // 2, S, 2 * LANES), BF16),
        jax.ShapeDtypeStruct((B, C_HEADS // 2, VT_ROWS, S), BF16),
    )
    slab = pl.BlockSpec((1, tm, LANES), lambda b, i: (b, i, 0))
    out_specs = (
        pl.BlockSpec((1, A_HEADS, tm, LANES), lambda b, i: (b, 0, i, 0)), slab,
        pl.BlockSpec((1, VT_ROWS, tm), lambda b, i: (b, 0, i)),
        pl.BlockSpec((1, B_HEADS, tm, LANES), lambda b, i: (b, 0, i, 0)), slab,
        pl.BlockSpec((1, VT_ROWS, tm), lambda b, i: (b, 0, i)),
        pl.BlockSpec((1, C_HEADS, tm, 2 * LANES), lambda b, i: (b, 0, i, 0)),
        pl.BlockSpec((1, C_HEADS // 2, tm, 2 * LANES), lambda b, i: (b, 0, i, 0)),
        pl.BlockSpec((1, C_HEADS // 2, VT_ROWS, tm), lambda b, i: (b, 0, 0, i)),
    )
    return pl.pallas_call(
        kern, out_shape=outs, grid=(B, S // tm),
        in_specs=[pl.BlockSpec((1, tm, D), lambda b, i: (b, i, 0)), mod, mod, full((1, D)), full((D, IN_PAD)),
                  tab, tab, tab, tab, tab, tab,
                  full((1, LANES)), full((1, LANES)), full((1, C_Q_LORA)), full((1, C_KV_LORA)),
                  full(wuq_pad.shape), full(wuk.shape), full(wuv.shape)],
        out_specs=out_specs,
        compiler_params=_cparams(("arbitrary", "arbitrary")),
        name="norm_inproj",
    )(x, sc1, sh1, g, w_pad, *tabs_a, *tabs_c, gq, gk, gcq, gckv, wuq_pad, wuk, wuv)


def _flash_kernel(q_ref, k_ref, vt_ref, e_ref, o_ref, qt_ref, m_ref, acc_ref, *, tk, cw, unroll):
    H, tq, dk = q_ref.shape[2:]
    S = k_ref.shape[2]
    R = H * tq
    nc = R // cw
    nsteps = S // tk
    for h in range(H):
        qt_ref[:, h * tq:(h + 1) * tq] = q_ref[0, 0, h].astype(F32).T.astype(BF16)
    s0 = jnp.dot(k_ref[0, 0, 0:ONES_ROWS, :], qt_ref[...], preferred_element_type=F32)
    m_ref[0] = jnp.max(s0, axis=0, keepdims=True)
    acc_ref[0] = jnp.zeros(acc_ref.shape[1:], F32)

    def step(j, src, dst, lagged):
        off = pl.multiple_of(j * tk, tk)
        k = k_ref[0, 0, pl.ds(off, tk), :]
        vt = vt_ref[0, 0, :, pl.ds(off, tk)]
        scores = lambda c: jnp.dot(k, qt_ref[:, c * cw:(c + 1) * cw], preferred_element_type=F32)
        accs, ms, jumps = [], [], []
        pending = [scores(c) for c in range(min(LOOKAHEAD, nc))]
        for c in range(nc):
            sl = slice(c * cw, (c + 1) * cw)
            s = pending.pop(0)
            if c + LOOKAHEAD < nc:
                pending.append(scores(c + LOOKAHEAD))
            m_prev = m_ref[src, :, sl]
            top = jnp.max(s, axis=0, keepdims=True)
            m_new = jnp.maximum(m_prev, top)
            alpha = jnp.exp2(m_prev - m_new)
            if lagged:
                p = jnp.exp2(s - m_prev).astype(BF16)
                accs.append(alpha * (acc_ref[src, :, sl] + jnp.dot(vt, p, preferred_element_type=F32)))
                jumps.append(top - m_prev)
            else:
                p = jnp.exp2(s - m_new).astype(BF16)
                accs.append(alpha * acc_ref[src, :, sl] + jnp.dot(vt, p, preferred_element_type=F32))
            ms.append(m_new)
        acc_ref[dst] = jnp.concatenate(accs, axis=1)
        m_ref[dst] = jnp.concatenate(ms, axis=1)
        return jnp.max(jnp.concatenate(jumps, axis=1)) if lagged else None

    def guarded_step(j, src, dst):
        jump = step(j, src, dst, True)

        @pl.when(jnp.logical_not(jump <= MAX_JUMP))
        def _():
            step(j, src, dst, False)

    def body(i, carry):
        guarded_step(2 * i, 0, 1)
        guarded_step(2 * i + 1, 1, 0)
        return carry

    assert nsteps % 2 == 0 or nsteps == 1
    if nsteps == 1:
        guarded_step(0, 0, 1)
    else:
        lax.fori_loop(0, nsteps // 2, body, 0, unroll=unroll)
    last = nsteps % 2
    o_t = acc_ref[last, 0:LANES, :] / acc_ref[last, LANES:LANES + 1, :]
    out = None
    for h in range(H):
        o_h = o_t[:, h * tq:(h + 1) * tq].T.astype(BF16)
        term = jnp.dot(o_h, e_ref[h], preferred_element_type=F32)
        out = term if out is None else out + term
    o_ref[0] = out.astype(BF16)


def _flash(q, k, vt, esel, *, tq, tk, cw=256, unroll=1):
    B, P, H, S, dk = q.shape
    W = esel.shape[2]
    tq = _tile(S, tq)
    tk = _tile(S, tk)
    R = H * tq
    cw = _tile(R, cw)
    return pl.pallas_call(
        functools.partial(_flash_kernel, tk=tk, cw=cw, unroll=unroll),
        out_shape=jax.ShapeDtypeStruct((B, S, P * W), BF16),
        grid=(B, P, S // tq),
        in_specs=[pl.BlockSpec((1, 1, H, tq, dk), lambda b, p, i: (b, p, 0, i, 0)),
                  pl.BlockSpec((1, 1, S, dk), lambda b, p, i: (b, p, 0, 0)),
                  pl.BlockSpec((1, 1, VT_ROWS, S), lambda b, p, i: (b, p, 0, 0)),
                  pl.BlockSpec((H, LANES, W), lambda b, p, i: (0, 0, 0))],
        out_specs=pl.BlockSpec((1, tq, W), lambda b, p, i: (b, i, p)),
        scratch_shapes=[pltpu.VMEM((dk, R), BF16), pltpu.VMEM((2, 1, R), F32), pltpu.VMEM((2, VT_ROWS, R), F32)],
        compiler_params=_cparams(("arbitrary", "arbitrary", "arbitrary")),
        name="flash_attn",
    )(q, k, vt, esel)


def _window_kernel(q_ref, k_ref, vt_ref, e_ref, slope_ref, sink_ref, o_ref, *, tq):
    H = q_ref.shape[1]
    S = k_ref.shape[1]
    wk = min(S, tq + 2 * WINDOW)
    i = pl.program_id(1)
    start = pl.multiple_of(jnp.clip(i * tq - WINDOW, 0, S - wk), WINDOW)
    k = k_ref[0, pl.ds(start, wk), :]
    vt = vt_ref[0, :, pl.ds(start, wk)]
    spos = start + lax.broadcasted_iota(jnp.int32, (wk, tq), 0)
    tpos = i * tq + lax.broadcasted_iota(jnp.int32, (wk, tq), 1)
    dist = jnp.abs(spos - tpos)
    inside = dist <= WINDOW
    distf = dist.astype(F32)
    qt = jnp.concatenate([q_ref[0, h].astype(F32).T.astype(BF16) for h in range(H)], axis=1)
    s_all = jnp.dot(k, qt, preferred_element_type=F32)
    es, ms = [], []
    for h in range(H):
        s = s_all[:, h * tq:(h + 1) * tq] - slope_ref[h] * distf
        s = jnp.where(inside, s, NEG_INF)
        m = jnp.maximum(jnp.max(s, axis=0, keepdims=True), sink_ref[h])
        es.append(jnp.exp2(s - m).astype(BF16))
        ms.append(m)
    acc = jnp.dot(vt, jnp.concatenate(es, axis=1), preferred_element_type=F32)
    out = None
    for h in range(H):
        sl = slice(h * tq, (h + 1) * tq)
        den = acc[LANES:LANES + 1, sl] + jnp.exp2(sink_ref[h] - ms[h])
        o = (acc[0:LANES, sl] / den).T.astype(BF16)
        term = jnp.dot(o, e_ref[h], preferred_element_type=F32)
        out = term if out is None else out + term
    o_ref[0] = out.astype(BF16)


def _window(q, k, v, esel, slopes, sink, *, tq):
    B, H, S, _ = q.shape
    W = esel.shape[2]
    tq = _tile(S, tq)
    return pl.pallas_call(
        functools.partial(_window_kernel, tq=tq),
        out_shape=jax.ShapeDtypeStruct((B, S, W), BF16),
        grid=(B, S // tq),
        in_specs=[pl.BlockSpec((1, H, tq, LANES), lambda b, i: (b, 0, i, 0)),
                  pl.BlockSpec((1, S, LANES), lambda b, i: (b, 0, 0)),
                  pl.BlockSpec((1, VT_ROWS, S), lambda b, i: (b, 0, 0)),
                  pl.BlockSpec((H, LANES, W), lambda b, i: (0, 0, 0)),
                  pl.BlockSpec((H, 1, 1), lambda b, i: (0, 0, 0)),
                  pl.BlockSpec((H, 1, 1), lambda b, i: (0, 0, 0))],
        out_specs=pl.BlockSpec((1, tq, W), lambda b, i: (b, i, 0)),
        compiler_params=_cparams(("arbitrary", "arbitrary")),
        name="window_attn",
    )(q, k, v, esel, slopes, sink)


def _sigmoid(v):
    return 1.0 / (1.0 + jnp.exp(-v))


def _pack_token_words(x):
    rp = x.shape[1] // (2 * LANES)
    words = []
    for r in range(rp):
        lo = lax.bitcast_convert_type(x[:, LANES * r:LANES * (r + 1)].astype(BF16).astype(F32), jnp.uint32)
        hi = lax.bitcast_convert_type(x[:, LANES * (rp + r):LANES * (rp + r + 1)].astype(BF16).astype(F32), jnp.uint32)
        words.append((lo >> 16) | hi)
    return words


def _load_token_words(ref, row0, n, rp):
    return [ref[pl.ds(row0 + r, n, stride=rp), :] for r in range(rp)]


def _unpack_token_words(words):
    lo = [lax.bitcast_convert_type(w << 16, F32) for w in words]
    hi = [lax.bitcast_convert_type(w & jnp.uint32(0xFFFF0000), F32) for w in words]
    return jnp.concatenate(lo + hi, axis=1)


def _post_kernel(oa_ref, ob_ref, oc_ref, x_ref, g1_ref, sc2_ref, sh2_ref, g2_ref, og_ref, wout_ref, n2_ref, rwt_ref,
                 sw1_ref, sw3_ref, sw2_ref, xs1_ref, u2_ref, st_ref):
    og = og_ref[...]

    def rn(o_ref, g):
        o = o_ref[0].astype(F32)
        return (o * lax.rsqrt(jnp.mean(o * o, axis=-1, keepdims=True) + EPS) * g).astype(BF16)

    mixed = jnp.concatenate([rn(oa_ref, og[:, :A_WIDTH]), rn(ob_ref, og[:, A_WIDTH:A_WIDTH + B_WIDTH]),
                             rn(oc_ref, og[:, A_WIDTH + B_WIDTH:])], axis=-1)
    x1 = x_ref[0] + g1_ref[0] * jnp.dot(mixed, wout_ref[...], preferred_element_type=F32)
    u2 = x1 * lax.rsqrt(jnp.mean(x1 * x1, axis=-1, keepdims=True) + EPS) * n2_ref[...]
    u2 = u2 * (1.0 + sc2_ref[0]) + sh2_ref[0]
    words = _pack_token_words(u2)
    for r, w in enumerate(words):
        u2_ref[pl.ds(r, u2.shape[0], stride=len(words)), :] = w
    ub = u2.astype(BF16)
    logits_t = lax.dot_general(rwt_ref[...], ub, (((1,), (1,)), ((), ())), preferred_element_type=F32)
    st_ref[...] = _sigmoid(logits_t)
    h1 = jnp.dot(ub, sw1_ref[...], preferred_element_type=F32)
    h3 = jnp.dot(ub, sw3_ref[...], preferred_element_type=F32)
    hid = (h1 * _sigmoid(h1) * h3).astype(BF16)
    xs1_ref[0] = x1 + g2_ref[0] * jnp.dot(hid, sw2_ref[...], preferred_element_type=F32)


def _post(oa, ob, oc, x, g1, sc2, sh2, g2, og, wout, n2, rwt, sw1, sw3, sw2):
    B, S, D = x.shape
    E = rwt.shape[0]
    tm = _tile(S, 512)
    nt = S // tm
    full = lambda a: pl.BlockSpec(a.shape, lambda b, i: (0,) * a.ndim)
    mod = pl.BlockSpec((1, 1, D), lambda b, i: (b, 0, 0))
    row = lambda w: pl.BlockSpec((1, tm, w), lambda b, i: (b, i, 0))
    return pl.pallas_call(
        _post_kernel,
        out_shape=(jax.ShapeDtypeStruct((B, S, D), F32),
                   jax.ShapeDtypeStruct((B * S * D // (2 * LANES), LANES), jnp.uint32),
                   jax.ShapeDtypeStruct((E, B * S), F32)),
        grid=(B, nt),
        in_specs=[row(A_WIDTH), row(B_WIDTH), row(C_WIDTH), row(D), mod, mod, mod, mod, full(og), full(wout), full(n2),
                  full(rwt), full(sw1), full(sw3), full(sw2)],
        out_specs=(row(D), pl.BlockSpec((tm * D // (2 * LANES), LANES), lambda b, i: (b * nt + i, 0)),
                   pl.BlockSpec((E, tm), lambda b, i: (0, b * nt + i))),
        compiler_params=_cparams(("arbitrary", "arbitrary")),
        name="post_attn",
    )(oa, ob, oc, x, g1, sc2, sh2, g2, og, wout, n2, rwt, sw1, sw3, sw2)


def _route_kernel(st_ref, rb_ref, tri_ref, idx_ref, w_ref, rank_ref, cnt_ref, carry_ref):
    E, tm = st_ref.shape
    per = E // N_GROUPS
    big = float(E)

    @pl.when(pl.program_id(0) == 0)
    def _():
        carry_ref[...] = jnp.zeros(carry_ref.shape, F32)

    s = st_ref[...]
    choice = s + rb_ref[...]
    c3 = choice.reshape(N_GROUPS, per, tm)
    io3 = lax.broadcasted_iota(jnp.int32, c3.shape, 1).astype(F32)
    m1 = jnp.max(c3, axis=1, keepdims=True)
    i1 = jnp.min(jnp.where(c3 == m1, io3, big), axis=1, keepdims=True)
    m2 = jnp.max(jnp.where(io3 == i1, -jnp.inf, c3), axis=1, keepdims=True)
    gs = (m1 + m2).reshape(N_GROUPS, tm)
    gio = lax.broadcasted_iota(jnp.int32, gs.shape, 0).astype(F32)
    keep = jnp.zeros(gs.shape, F32)
    for _ in range(TOPK_GROUPS):
        mg = jnp.max(gs, axis=0, keepdims=True)
        ig = jnp.min(jnp.where(gs == mg, gio, big), axis=0, keepdims=True)
        sel = gio == ig
        keep = jnp.where(sel, 1.0, keep)
        gs = jnp.where(sel, -jnp.inf, gs)
    cur = jnp.where(keep.reshape(N_GROUPS, 1, tm) > 0.5, c3, NEG_INF).reshape(E, tm)
    eio = lax.broadcasted_iota(jnp.int32, (E, tm), 0).astype(F32)
    sels, idxs, ws = [], [], []
    for _ in range(TOP_K):
        m = jnp.max(cur, axis=0, keepdims=True)
        ik = jnp.min(jnp.where(cur == m, eio, big), axis=0, keepdims=True)
        sel = eio == ik
        ws.append(jnp.sum(jnp.where(sel, s, 0.0), axis=0, keepdims=True))
        cur = jnp.where(sel, -jnp.inf, cur)
        sels.append(sel)
        idxs.append(ik)
    wsum = ws[0]
    for wk in ws[1:]:
        wsum = wsum + wk
    w_ref[...] = jnp.concatenate([wk / wsum * ROUTED_SCALE for wk in ws], axis=0)
    idx_ref[...] = jnp.concatenate(idxs, axis=0).astype(jnp.int32)
    onehot = jnp.zeros((E, tm), F32)
    for sel in sels:
        onehot = jnp.where(sel, 1.0, onehot)
    prefix = jnp.dot(onehot.astype(BF16), tri_ref[...], preferred_element_type=F32) + carry_ref[...]
    rank_ref[...] = jnp.concatenate(
        [jnp.sum(jnp.where(sel, prefix, 0.0), axis=0, keepdims=True) for sel in sels], axis=0).astype(jnp.int32)
    carry = carry_ref[...] + jnp.sum(onehot, axis=1, keepdims=True)
    carry_ref[...] = carry
    cnt_ref[...] = jnp.broadcast_to(carry, cnt_ref.shape)


def _route(scores_t, router_b):
    E, N = scores_t.shape
    tm = _tile(N, 512)
    tri = (jnp.arange(tm)[:, None] < jnp.arange(tm)[None, :]).astype(BF16)
    return pl.pallas_call(
        _route_kernel,
        out_shape=(jax.ShapeDtypeStruct((TOP_K, N), jnp.int32), jax.ShapeDtypeStruct((TOP_K, N), F32),
                   jax.ShapeDtypeStruct((TOP_K, N), jnp.int32), jax.ShapeDtypeStruct((E, LANES), F32)),
        grid=(N // tm,),
        in_specs=[pl.BlockSpec((E, tm), lambda i: (0, i)), pl.BlockSpec((E, 1), lambda i: (0, 0)),
                  pl.BlockSpec((tm, tm), lambda i: (0, 0))],
        out_specs=(pl.BlockSpec((TOP_K, tm), lambda i: (0, i)), pl.BlockSpec((TOP_K, tm), lambda i: (0, i)),
                   pl.BlockSpec((TOP_K, tm), lambda i: (0, i)), pl.BlockSpec((E, LANES), lambda i: (0, 0))),
        scratch_shapes=[pltpu.VMEM((E, 1), F32)],
        compiler_params=_cparams(("arbitrary",)),
        name="route_topk",
    )(scores_t, router_b.reshape(E, 1), tri)


def _slots_kernel(idx_ref, rank_ref, starts_ref, dest_ref):
    E = starts_ref.shape[0]
    tm = idx_ref.shape[1]
    eio = lax.broadcasted_iota(jnp.int32, (E, tm), 0)
    starts = starts_ref[...]
    rows = [jnp.sum(jnp.where(eio == idx_ref[k:k + 1, :], starts, 0.0), axis=0, keepdims=True) for k in range(TOP_K)]
    dest_ref[...] = jnp.concatenate(rows, axis=0).astype(jnp.int32) + rank_ref[...]


def _slots(idx, rank, starts):
    K, N = idx.shape
    E = starts.shape[0]
    tm = _tile(N, 512)
    blk = pl.BlockSpec((K, tm), lambda i: (0, i))
    return pl.pallas_call(
        _slots_kernel, out_shape=jax.ShapeDtypeStruct((K, N), jnp.int32), grid=(N // tm,),
        in_specs=[blk, blk, pl.BlockSpec((E, 1), lambda i: (0, 0))], out_specs=blk,
        compiler_params=_cparams(("arbitrary",)), name="moe_slots",
    )(idx, rank, starts.astype(F32).reshape(E, 1))


TOKENS_PER_ISSUE = 8


def _dispatch_kernel(dest_ref, u_ref, xs_ref, sem, *, ch):
    tm = u_ref.shape[0] // ch

    def body(g, carry):
        base = pl.multiple_of(g * (TOKENS_PER_ISSUE * ch), TOKENS_PER_ISSUE * ch)
        for r in range(TOKENS_PER_ISSUE):
            for k in range(TOP_K):
                d = dest_ref[g * (TOKENS_PER_ISSUE * TOP_K) + r * TOP_K + k]
                pltpu.make_async_copy(u_ref.at[pl.ds(base + r * ch, ch)],
                                      xs_ref.at[pl.ds(pl.multiple_of(d * ch, ch), ch)],
                                      sem).start(priority=k % 2)
        return carry

    lax.fori_loop(0, tm // TOKENS_PER_ISSUE, body, 0)
    for k in range(TOP_K):
        pltpu.make_async_copy(u_ref, xs_ref.at[pl.ds(0, tm * ch)], sem).wait()


def _dispatch(dest_flat, u2t, n_slots, ch):
    N = u2t.shape[0] // ch
    tm = _tile(N, 256)
    return pl.pallas_call(
        functools.partial(_dispatch_kernel, ch=ch),
        out_shape=jax.ShapeDtypeStruct((n_slots * ch, LANES), u2t.dtype),
        grid=(N // tm,),
        in_specs=[pl.BlockSpec((tm * TOP_K,), lambda i: (i,), memory_space=pltpu.SMEM),
                  pl.BlockSpec((tm * ch, LANES), lambda i: (i, 0))],
        out_specs=pl.BlockSpec(memory_space=pl.ANY),
        scratch_shapes=[pltpu.SemaphoreType.DMA(())],
        compiler_params=pltpu.CompilerParams(dimension_semantics=("arbitrary",), vmem_limit_bytes=VMEM_LIMIT,
                                             has_side_effects=True),
        name="moe_dispatch",
    )(dest_flat, u2t)


def _expert_kernel(wb_ref, we_ref, lo_ref, hi_ref, nw_ref, xs_ref, w1_ref, w3_ref, w2_ref, ys_ref, w1b, w3b, w2b):
    j = pl.program_id(0)

    @pl.when(j < nw_ref[0])
    def _():
        prev = jnp.maximum(j - 1, 0)
        new_expert = jnp.logical_or(j == 0, we_ref[j] != we_ref[prev])
        new_block = jnp.logical_or(j == 0, wb_ref[j] != wb_ref[prev])

        @pl.when(new_expert)
        def _():
            w1b[...] = w1_ref[0].astype(BF16)
            w3b[...] = w3_ref[0].astype(BF16)
            w2b[...] = w2_ref[0].astype(BF16)

        rp = w1b.shape[0] // (2 * LANES)
        rows = lax.broadcasted_iota(jnp.int32, (MOE_BM, LANES), 0)
        mine = (rows >= lo_ref[j]) & (rows < hi_ref[j])
        words = [[] for _ in range(rp)]
        for s in range(MOE_SPLIT):
            r0, nr = s * (MOE_BM // MOE_SPLIT), MOE_BM // MOE_SPLIT
            xw = [jnp.where(mine[r0:r0 + nr], w, jnp.uint32(0)) for w in _load_token_words(xs_ref, r0 * rp, nr, rp)]
            xb = _unpack_token_words(xw).astype(BF16)
            h1 = jnp.dot(xb, w1b[...], preferred_element_type=F32)
            h3 = jnp.dot(xb, w3b[...], preferred_element_type=F32)
            hid = (h1 * _sigmoid(h1) * h3).astype(BF16)
            for r, w in enumerate(_pack_token_words(jnp.dot(hid, w2b[...], preferred_element_type=F32))):
                words[r].append(w)
        words = [jnp.concatenate(ws, axis=0) for ws in words]

        @pl.when(new_block)
        def _():
            for r, w in enumerate(words):
                ys_ref[pl.ds(r, MOE_BM, stride=rp), :] = w

        @pl.when(jnp.logical_not(new_block))
        def _():
            for r, prev in enumerate(_load_token_words(ys_ref, 0, MOE_BM, rp)):
                ys_ref[pl.ds(r, MOE_BM, stride=rp), :] = jnp.where(mine, words[r], prev)


def _experts(wb, we, lo, hi, nw, xs, w1, w3, w2, *, layer):
    D, F = w1.shape[2:]
    rows = MOE_BM * D // (2 * LANES)
    blk = lambda j, wb, we, lo, hi, nw: (wb[j], 0)
    wsel = lambda j, wb, we, lo, hi, nw: (layer, we[j], 0, 0)
    grid_spec = pltpu.PrefetchScalarGridSpec(
        num_scalar_prefetch=5, grid=(wb.shape[0],),
        in_specs=[pl.BlockSpec((rows, LANES), blk), pl.BlockSpec((None, 1, D, F), wsel),
                  pl.BlockSpec((None, 1, D, F), wsel), pl.BlockSpec((None, 1, F, D), wsel)],
        out_specs=pl.BlockSpec((rows, LANES), blk),
        scratch_shapes=[pltpu.VMEM((D, F), BF16), pltpu.VMEM((D, F), BF16), pltpu.VMEM((F, D), BF16)])
    return pl.pallas_call(
        _expert_kernel, out_shape=jax.ShapeDtypeStruct(xs.shape, xs.dtype), grid_spec=grid_spec,
        compiler_params=_cparams(("arbitrary",)), name="moe_experts",
    )(wb, we, lo, hi, nw, xs, w1, w3, w2)


def _combine_kernel(dcur_ref, dnext_ref, w_ref, xs1_ref, g2_ref, fg_ref, ys_ref, o_ref, buf, sems, *, final):
    tm, D = xs1_ref.shape
    ch = D // (2 * LANES)
    i = pl.program_id(0)
    n = pl.num_programs(0)

    def issue(dref, slot):
        def body(g, carry):
            base = pl.multiple_of(g * (TOKENS_PER_ISSUE * ch), TOKENS_PER_ISSUE * ch)
            for r in range(TOKENS_PER_ISSUE):
                for k in range(TOP_K):
                    d = dref[g * (TOKENS_PER_ISSUE * TOP_K) + r * TOP_K + k]
                    pltpu.make_async_copy(ys_ref.at[pl.ds(pl.multiple_of(d * ch, ch), ch)],
                                          buf.at[slot, k, pl.ds(base + r * ch, ch)],
                                          sems.at[slot]).start(priority=k % 2)
            return carry

        lax.fori_loop(0, tm // TOKENS_PER_ISSUE, body, 0)

    def finish(slot):
        for k in range(TOP_K):
            pltpu.make_async_copy(ys_ref.at[pl.ds(0, tm * ch)], buf.at[slot, k], sems.at[slot]).wait()
        w = w_ref[...]
        r = None
        for k in range(TOP_K):
            term = w[:, k:k + 1] * _unpack_token_words(_load_token_words(buf.at[slot, k], 0, tm, ch))
            r = term if r is None else r + term
        x2 = xs1_ref[...] + g2_ref[0] * r
        if final:
            x2 = x2 * lax.rsqrt(jnp.mean(x2 * x2, axis=-1, keepdims=True) + EPS) * fg_ref[...]
        o_ref[...] = x2

    @pl.when(i == 0)
    def _():
        issue(dcur_ref, 0)

    for slot in range(2):
        @pl.when(jnp.logical_and(i + 1 < n, (i + 1) % 2 == slot))
        def _():
            issue(dnext_ref, slot)

    for slot in range(2):
        @pl.when(i % 2 == slot)
        def _():
            finish(slot)


def _combine(dest_flat, w_tok, xs1, g2, fg, ys, *, seq, final):
    N, D = xs1.shape
    ch = D // (2 * LANES)
    tm = _tile(seq, 128)
    per_b = seq // tm
    n = N // tm
    return pl.pallas_call(
        functools.partial(_combine_kernel, final=final),
        out_shape=jax.ShapeDtypeStruct((N, D), F32),
        grid=(n,),
        in_specs=[pl.BlockSpec((tm * TOP_K,), lambda i: (i,), memory_space=pltpu.SMEM),
                  pl.BlockSpec((tm * TOP_K,), lambda i: (jnp.minimum(i + 1, n - 1),), memory_space=pltpu.SMEM),
                  pl.BlockSpec((tm, TOP_K), lambda i: (i, 0)),
                  pl.BlockSpec((tm, D), lambda i: (i, 0)),
                  pl.BlockSpec((1, 1, D), lambda i: (i // per_b, 0, 0)),
                  pl.BlockSpec((1, D), lambda i: (0, 0)),
                  pl.BlockSpec(memory_space=pl.ANY)],
        out_specs=pl.BlockSpec((tm, D), lambda i: (i, 0)),
        scratch_shapes=[pltpu.VMEM((2, TOP_K, tm * ch, LANES), jnp.uint32), pltpu.SemaphoreType.DMA((2,))],
        compiler_params=_cparams(("arbitrary",)),
        name="moe_combine",
    )(dest_flat, dest_flat, w_tok, xs1, g2, fg, ys)


def _rope_tables(S):
    half = HEAD_DIM // 2
    inv = ROPE_THETA ** (-jnp.arange(0, half, 2, dtype=F32) / half)
    t = jnp.arange(S, dtype=jnp.int32)

    def part(pos):
        ang = pos.astype(F32)[:, None] * inv[None, :]
        c, s, z = jnp.cos(ang), jnp.sin(ang), jnp.zeros_like(ang)
        return jnp.concatenate([c, c], -1), jnp.concatenate([-s, z], -1), jnp.concatenate([z, s], -1)

    row, col, lin = part(t // GRID_W), part(t % GRID_W), part(t)
    tabs_a = tuple(jnp.tile(jnp.concatenate([r, c], -1), (1, LANES // HEAD_DIM)) for r, c in zip(row, col))
    tabs_c = tuple(jnp.pad(v, ((0, 0), (0, LANES - C_ROPE))) for v in lin)
    return tabs_a, tabs_c


def _head_slabs(w, heads, groups):
    d = w.shape[0]
    hot = (jnp.arange(heads)[:, None] // (heads // groups) == jnp.arange(groups)[None, :]).astype(w.dtype)
    return (w.reshape(d, heads, 1, HEAD_DIM) * hot[None, :, :, None]).reshape(d, heads * LANES)


def _pad_w_in(w):
    d = w.shape[0]
    a, b, c = w[:, :A_IN], w[:, A_IN:A_IN + B_IN], w[:, A_IN + B_IN:]
    kr = jnp.pad(c[:, C_Q_LORA + C_KV_LORA:], ((0, 0), (0, LANES - C_ROPE)))
    return jnp.concatenate([_head_slabs(a[:, :A_WIDTH], A_HEADS, A_KV_HEADS), a[:, A_WIDTH:],
                            _head_slabs(b[:, :B_WIDTH], B_HEADS, B_KV_HEADS), b[:, B_WIDTH:],
                            c[:, :C_Q_LORA + C_KV_LORA], kr], axis=1).astype(BF16)


def _pad_w_uq(w):
    r = w.shape[0]
    w = w.reshape(r, C_HEADS, C_NOPE + C_ROPE)
    hot = (jnp.arange(C_HEADS)[:, None] % 2 == jnp.arange(2)[None, :]).astype(w.dtype)
    nope = (w[:, :, None, :C_NOPE] * hot[None, :, :, None]).reshape(r, C_HEADS, LANES)
    rope = jnp.pad(w[:, :, C_NOPE:], ((0, 0), (0, 0), (0, LANES - C_ROPE)))
    return jnp.concatenate([nope, rope], axis=-1).reshape(r, C_HEADS * 2 * LANES).astype(BF16)


def _select_mats(heads, groups, out_w):
    h = jnp.arange(heads)[:, None, None]
    r = jnp.arange(LANES)[None, :, None]
    c = jnp.arange(out_w)[None, None, :]
    half = h // (heads // groups)
    return ((r // HEAD_DIM == half) & (c == HEAD_DIM * h + r % HEAD_DIM)).astype(BF16)


def kernel(x, c, mod_w, mod_b, norm1_g, norm2_g, w_in, a_qnorm_g, a_knorm_g, b_sink, c_qnorm_g, c_kvnorm_g, c_w_uq, c_w_uk, c_w_uv, out_norm_g, w_out, router_w, router_b, exp_w1, exp_w3, exp_w2, sh_w1, sh_w3, sh_w2, final_g):
    B, S, D = x.shape
    L = mod_w.shape[0]
    E = router_w.shape[2]
    N = B * S
    n_slots = N * TOP_K
    assert n_slots % MOE_BM == 0
    n_work = n_slots // MOE_BM + E

    mod = _mod(c, mod_w, mod_b).reshape(L, B, 6, 1, D)
    tabs_a, tabs_c = _rope_tables(S)
    esel_a = _select_mats(A_HEADS, A_KV_HEADS, A_WIDTH)
    esel_c = _select_mats(2, 2, LANES)
    slopes = (jnp.exp2(-8.0 * jnp.arange(1, B_HEADS + 1, dtype=F32) / B_HEADS) * LOG2E).reshape(B_HEADS, 1, 1)

    for l in range(L):
        sh1, sc1, g1, sh2, sc2, g2 = [mod[l, :, i] for i in range(6)]
        gq2 = jnp.tile(a_qnorm_g[l], LANES // HEAD_DIM).reshape(1, LANES)
        gk2 = jnp.tile(a_knorm_g[l], LANES // HEAD_DIM).reshape(1, LANES)
        qa, ka, va, qb, kb, vb, qc, kc, vc = _inproj(
            x, sc1, sh1, norm1_g[l].reshape(1, D), _pad_w_in(w_in[l]), tabs_a, tabs_c, gq2, gk2,
            c_qnorm_g[l].reshape(1, -1), c_kvnorm_g[l].reshape(1, -1), _pad_w_uq(c_w_uq[l]),
            c_w_uk[l].astype(BF16), c_w_uv[l].astype(BF16))
        oa = _flash(qa[:, None], ka[:, None], va[:, None], esel_a, tq=1024, tk=512, cw=512, unroll=1)
        ob = _window(qb, kb, vb, esel_a, slopes, (b_sink[l].astype(F32) * LOG2E).reshape(B_HEADS, 1, 1), tq=256)
        oc = _flash(qc.reshape(B, C_HEADS // 2, 2, S, 2 * LANES), kc, vc, esel_c, tq=2048, tk=512, cw=512, unroll=1)
        xs1, u2, scores_t = _post(
            oa, ob, oc, x, g1, sc2, sh2, g2, out_norm_g[l].reshape(1, -1), w_out[l].astype(BF16),
            norm2_g[l].reshape(1, D), router_w[l].T.astype(BF16), sh_w1[l].astype(BF16), sh_w3[l].astype(BF16),
            sh_w2[l].astype(BF16))
        idx, w_top, rank, cnt = _route(scores_t, router_b[l])
        counts = cnt[:, 0].astype(jnp.int32)
        ends = jnp.cumsum(counts)
        starts = ends - counts
        dest = _slots(idx, rank, starts).T.reshape(N * TOP_K)
        first_blk = starts // MOE_BM
        n_items = jnp.where(counts > 0, (ends - 1) // MOE_BM - first_blk + 1, 0)
        item_ends = jnp.cumsum(n_items)
        nw = item_ends[-1].reshape(1).astype(jnp.int32)
        wid = jnp.minimum(jnp.arange(n_work, dtype=jnp.int32), nw[0] - 1)
        we = jnp.minimum(jnp.sum(item_ends[None, :] <= wid[:, None], axis=1), E - 1).astype(jnp.int32)
        hot = we[:, None] == jnp.arange(E, dtype=jnp.int32)[None, :]
        take = lambda tbl: jnp.sum(jnp.where(hot, tbl[None, :], 0), axis=1).astype(jnp.int32)
        wb = take(first_blk) + wid - take(item_ends - n_items)
        lo = jnp.clip(take(starts) - wb * MOE_BM, 0, MOE_BM).astype(jnp.int32)
        hi = jnp.clip(take(ends) - wb * MOE_BM, 0, MOE_BM).astype(jnp.int32)
        xs = _dispatch(dest, u2, n_slots, D // (2 * LANES))
        ys = _experts(wb, we, lo, hi, nw, xs, exp_w1, exp_w3, exp_w2, layer=l)
        x = _combine(dest, w_top.T, xs1.reshape(N, D), g2, final_g.reshape(1, D), ys, seq=S,
                     final=(l == L - 1)).reshape(B, S, D)
    return x
```

```python
import functools

import jax
import jax.numpy as jnp
from jax import lax
from jax.experimental import pallas as pl
from jax.experimental.pallas import tpu as pltpu

F32 = jnp.float32
BF16 = jnp.bfloat16

HEAD_DIM = 64
GRID_W = 64
ROPE_THETA = 10000.0
EPS = 1e-6
NEG_INF = -1e30
A_HEADS, A_KV_HEADS = 6, 2
B_HEADS, B_KV_HEADS = 6, 2
WINDOW = 128
C_HEADS, C_Q_LORA, C_KV_LORA, C_NOPE, C_ROPE, C_V = 4, 256, 128, 64, 32, 64
A_WIDTH = A_HEADS * HEAD_DIM
B_WIDTH = B_HEADS * HEAD_DIM
C_WIDTH = C_HEADS * C_V
A_IN = (A_HEADS + 2 * A_KV_HEADS) * HEAD_DIM
B_IN = (B_HEADS + 2 * B_KV_HEADS) * HEAD_DIM
TOP_K = 8
N_GROUPS = 8
TOPK_GROUPS = 4
ROUTED_SCALE = 2.5
LOG2E = 1.4426950408889634

LANES = 128
SUBLANE_BITS = 3
SUBLANES = 1 << SUBLANE_BITS
OFF_AQ = 0
OFF_AK = OFF_AQ + A_HEADS * LANES
OFF_AV = OFF_AK + LANES
OFF_BQ = OFF_AV + LANES
OFF_BK = OFF_BQ + B_HEADS * LANES
OFF_BV = OFF_BK + LANES
OFF_CQ = OFF_BV + LANES
OFF_CKV = OFF_CQ + C_Q_LORA
OFF_CKR = OFF_CKV + C_KV_LORA
IN_PAD = OFF_CKR + LANES

ONES_ROWS = 16
VT_ROWS = LANES + ONES_ROWS
LOOKAHEAD = 3
MAX_JUMP = 64.0
MOE_BM = 512
MOE_SPLIT = 2
VMEM_LIMIT = 56 * 1024 * 1024


def _tile(n, pref):
    t = min(n, pref)
    assert n % t == 0, (n, t)
    return t


def _cparams(sem, vmem=VMEM_LIMIT):
    return pltpu.CompilerParams(dimension_semantics=sem, vmem_limit_bytes=vmem)


def _mod_kernel(c_ref, w_ref, b_ref, o_ref):
    c = c_ref[...]
    ca = c * (1.0 / (1.0 + jnp.exp(-c)))
    o_ref[0] = jnp.dot(ca, w_ref[0], preferred_element_type=F32, precision=lax.Precision.HIGHEST) + b_ref[0]


def _mod(c, mod_w, mod_b):
    L, D, W = mod_w.shape
    B = c.shape[0]
    tn = _tile(W, 1536)
    return pl.pallas_call(
        _mod_kernel,
        out_shape=jax.ShapeDtypeStruct((L, B, W), F32),
        grid=(L, W // tn),
        in_specs=[
            pl.BlockSpec((B, D), lambda l, j: (0, 0)),
            pl.BlockSpec((1, D, tn), lambda l, j: (l, 0, j)),
            pl.BlockSpec((1, 1, tn), lambda l, j: (l, 0, j)),
        ],
        out_specs=pl.BlockSpec((1, B, tn), lambda l, j: (l, 0, j)),
        compiler_params=_cparams(("arbitrary", "arbitrary")),
        name="mod_proj",
    )(c, mod_w, mod_b.reshape(L, 1, W))


def _rope(v, c, sa, sb):
    return v * c + pltpu.roll(v, LANES - 16, 1) * sa + pltpu.roll(v, 16, 1) * sb


def _inproj_kernel(x_ref, sc_ref, sh_ref, g_ref, w_ref, ca_ref, saa_ref, sba_ref, cc_ref, sac_ref, sbc_ref,
                   gq_ref, gk_ref, gcq_ref, gckv_ref, wuq_ref, wuk_ref, wuv_ref,
                   qa_ref, ka_ref, va_ref, qb_ref, kb_ref, vb_ref, qc_ref, kc_ref, vc_ref, *, qs_ab, qs_c):
    tm = x_ref.shape[1]
    nh = 2 if tm % (2 * LANES) == 0 else 1
    hm = tm // nh

    def part(r0):
        rows = slice(r0, r0 + hm)
        x = x_ref[0, rows, :]
        u = x * lax.rsqrt(jnp.mean(x * x, axis=-1, keepdims=True) + EPS) * g_ref[...]
        u = u * (1.0 + sc_ref[0]) + sh_ref[0]
        p = jnp.dot(u.astype(BF16), w_ref[...], preferred_element_type=F32)
        ca, saa, sba = ca_ref[rows, :], saa_ref[rows, :], sba_ref[rows, :]
        cc, sac, sbc = cc_ref[rows, :], sac_ref[rows, :], sbc_ref[rows, :]
        out = {}
        gq = gq_ref[...]
        qa = []
        for h in range(A_HEADS):
            v = p[:, OFF_AQ + LANES * h:OFF_AQ + LANES * (h + 1)]
            ss = jnp.sum(v * v, axis=-1, keepdims=True) * (1.0 / HEAD_DIM)
            v = v * lax.rsqrt(ss + EPS) * gq
            qa.append((_rope(v, ca, saa, sba) * qs_ab).astype(BF16))
        out["qa"] = qa
        k = p[:, OFF_AK:OFF_AK + LANES]
        lane = lax.broadcasted_iota(jnp.int32, k.shape, 1)
        k2 = k * k
        s0 = jnp.sum(jnp.where(lane < HEAD_DIM, k2, 0.0), axis=-1, keepdims=True)
        s1 = jnp.sum(k2, axis=-1, keepdims=True) - s0
        r = jnp.where(lane < HEAD_DIM, lax.rsqrt(s0 * (1.0 / HEAD_DIM) + EPS), lax.rsqrt(s1 * (1.0 / HEAD_DIM) + EPS))
        out["ka"] = _rope(k * r * gk_ref[...], ca, saa, sba).astype(BF16)
        out["va"] = p[:, OFF_AV:OFF_AV + LANES].T.astype(BF16)
        out["qb"] = [(p[:, OFF_BQ + LANES * h:OFF_BQ + LANES * (h + 1)] * qs_ab).astype(BF16) for h in range(B_HEADS)]
        out["kb"] = p[:, OFF_BK:OFF_BK + LANES].astype(BF16)
        out["vb"] = p[:, OFF_BV:OFF_BV + LANES].T.astype(BF16)
        cq = p[:, OFF_CQ:OFF_CQ + C_Q_LORA]
        cq = cq * lax.rsqrt(jnp.mean(cq * cq, axis=-1, keepdims=True) + EPS) * gcq_ref[...]
        qh = jnp.dot(cq.astype(BF16), wuq_ref[...], preferred_element_type=F32)
        qc = []
        for h in range(C_HEADS):
            base = 2 * LANES * h
            qc.append(jnp.concatenate(
                [(qh[:, base:base + LANES] * qs_c).astype(BF16),
                 (_rope(qh[:, base + LANES:base + 2 * LANES], cc, sac, sbc) * qs_c).astype(BF16)], axis=1))
        out["qc"] = qc
        ckv = p[:, OFF_CKV:OFF_CKV + C_KV_LORA]
        ckv = (ckv * lax.rsqrt(jnp.mean(ckv * ckv, axis=-1, keepdims=True) + EPS) * gckv_ref[...]).astype(BF16)
        kn = jnp.dot(ckv, wuk_ref[...], preferred_element_type=F32)
        vv = jnp.dot(ckv, wuv_ref[...], preferred_element_type=F32)
        kr = _rope(p[:, OFF_CKR:OFF_CKR + LANES], cc, sac, sbc).astype(BF16)
        out["kc"] = [jnp.concatenate([kn[:, LANES * pr:LANES * (pr + 1)].astype(BF16), kr], axis=1)
                     for pr in range(C_HEADS // 2)]
        out["vc"] = [vv[:, LANES * pr:LANES * (pr + 1)].T.astype(BF16) for pr in range(C_HEADS // 2)]
        return out

    parts = [part(i * hm) for i in range(nh)]
    rowcat = lambda f: jnp.concatenate([f(o) for o in parts], axis=0)
    ones = jnp.ones((ONES_ROWS, tm), BF16)
    colcat = lambda f: jnp.concatenate([jnp.concatenate([f(o) for o in parts], axis=1), ones], axis=0)
    for h in range(A_HEADS):
        qa_ref[0, h] = rowcat(lambda o: o["qa"][h])
    ka_ref[0] = rowcat(lambda o: o["ka"])
    va_ref[0] = colcat(lambda o: o["va"])
    for h in range(B_HEADS):
        qb_ref[0, h] = rowcat(lambda o: o["qb"][h])
    kb_ref[0] = rowcat(lambda o: o["kb"])
    vb_ref[0] = colcat(lambda o: o["vb"])
    for h in range(C_HEADS):
        qc_ref[0, h] = rowcat(lambda o: o["qc"][h])
    for pr in range(C_HEADS // 2):
        kc_ref[0, pr] = rowcat(lambda o: o["kc"][pr])
        vc_ref[0, pr] = colcat(lambda o: o["vc"][pr])


def _inproj(x, sc1, sh1, g, w_pad, tabs_a, tabs_c, gq, gk, gcq, gckv, wuq_pad, wuk, wuv):
    B, S, D = x.shape
    tm = _tile(S, 512)
    full = lambda shp: pl.BlockSpec(shp, lambda b, i: (0,) * len(shp))
    tab = pl.BlockSpec((tm, LANES), lambda b, i: (i, 0))
    mod = pl.BlockSpec((1, 1, D), lambda b, i: (b, 0, 0))
    kern = functools.partial(_inproj_kernel, qs_ab=HEAD_DIM ** -0.5 * LOG2E, qs_c=(C_NOPE + C_ROPE) ** -0.5 * LOG2E)
    outs = (
        jax.ShapeDtypeStruct((B, A_HEADS, S, LANES), BF16), jax.ShapeDtypeStruct((B, S, LANES), BF16),
        jax.ShapeDtypeStruct((B, VT_ROWS, S), BF16),
        jax.ShapeDtypeStruct((B, B_HEADS, S, LANES), BF16), jax.ShapeDtypeStruct((B, S, LANES), BF16),
        jax.ShapeDtypeStruct((B, VT_ROWS, S), BF16),
        jax.ShapeDtypeStruct((B, C_HEADS, S, 2 * LANES), BF16), jax.ShapeDtypeStruct((B, C_HEADS // 2, S, 2 * LANES), BF16),
        jax.ShapeDtypeStruct((B, C_HEADS // 2, VT_ROWS, S), BF16),
    )
    slab = pl.BlockSpec((1, tm, LANES), lambda b, i: (b, i, 0))
    out_specs = (
        pl.BlockSpec((1, A_HEADS, tm, LANES), lambda b, i: (b, 0, i, 0)), slab,
        pl.BlockSpec((1, VT_ROWS, tm), lambda b, i: (b, 0, i)),
        pl.BlockSpec((1, B_HEADS, tm, LANES), lambda b, i: (b, 0, i, 0)), slab,
        pl.BlockSpec((1, VT_ROWS, tm), lambda b, i: (b, 0, i)),
        pl.BlockSpec((1, C_HEADS, tm, 2 * LANES), lambda b, i: (b, 0, i, 0)),
        pl.BlockSpec((1, C_HEADS // 2, tm, 2 * LANES), lambda b, i: (b, 0, i, 0)),
        pl.BlockSpec((1, C_HEADS // 2, VT_ROWS, tm), lambda b, i: (b, 0, 0, i)),
    )
    return pl.pallas_call(
        kern, out_shape=outs, grid=(B, S // tm),
        in_specs=[pl.BlockSpec((1, tm, D), lambda b, i: (b, i, 0)), mod, mod, full((1, D)), full((D, IN_PAD)),
                  tab, tab, tab, tab, tab, tab,
                  full((1, LANES)), full((1, LANES)), full((1, C_Q_LORA)), full((1, C_KV_LORA)),
                  full(wuq_pad.shape), full(wuk.shape), full(wuv.shape)],
        out_specs=out_specs,
        compiler_params=_cparams(("arbitrary", "arbitrary")),
        name="norm_inproj",
    )(x, sc1, sh1, g, w_pad, *tabs_a, *tabs_c, gq, gk, gcq, gckv, wuq_pad, wuk, wuv)


def _flash_kernel(q_ref, k_ref, vt_ref, e_ref, o_ref, qt_ref, m_ref, acc_ref, *, tk, cw, unroll):
    H, tq, dk = q_ref.shape[2:]
    S = k_ref.shape[2]
    R = H * tq
    nc = R // cw
    nsteps = S // tk
    for h in range(H):
        qt_ref[:, h * tq:(h + 1) * tq] = q_ref[0, 0, h].astype(F32).T.astype(BF16)
    s0 = jnp.dot(k_ref[0, 0, 0:ONES_ROWS, :], qt_ref[...], preferred_element_type=F32)
    m_ref[0] = jnp.max(s0, axis=0, keepdims=True)
    acc_ref[0] = jnp.zeros(acc_ref.shape[1:], F32)

    def step(j, src, dst, lagged):
        off = pl.multiple_of(j * tk, tk)
        k = k_ref[0, 0, pl.ds(off, tk), :]
        vt = vt_ref[0, 0, :, pl.ds(off, tk)]
        scores = lambda c: jnp.dot(k, qt_ref[:, c * cw:(c + 1) * cw], preferred_element_type=F32)
        accs, ms, jumps = [], [], []
        pending = [scores(c) for c in range(min(LOOKAHEAD, nc))]
        for c in range(nc):
            sl = slice(c * cw, (c + 1) * cw)
            s = pending.pop(0)
            if c + LOOKAHEAD < nc:
                pending.append(scores(c + LOOKAHEAD))
            m_prev = m_ref[src, :, sl]
            top = jnp.max(s, axis=0, keepdims=True)
            m_new = jnp.maximum(m_prev, top)
            alpha = jnp.exp2(m_prev - m_new)
            if lagged:
                p = jnp.exp2(s - m_prev).astype(BF16)
                accs.append(alpha * (acc_ref[src, :, sl] + jnp.dot(vt, p, preferred_element_type=F32)))
                jumps.append(top - m_prev)
            else:
                p = jnp.exp2(s - m_new).astype(BF16)
                accs.append(alpha * acc_ref[src, :, sl] + jnp.dot(vt, p, preferred_element_type=F32))
            ms.append(m_new)
        acc_ref[dst] = jnp.concatenate(accs, axis=1)
        m_ref[dst] = jnp.concatenate(ms, axis=1)
        return jnp.max(jnp.concatenate(jumps, axis=1)) if lagged else None

    def guarded_step(j, src, dst):
        jump = step(j, src, dst, True)

        @pl.when(jnp.logical_not(jump <= MAX_JUMP))
        def _():
            step(j, src, dst, False)

    def body(i, carry):
        guarded_step(2 * i, 0, 1)
        guarded_step(2 * i + 1, 1, 0)
        return carry

    assert nsteps % 2 == 0 or nsteps == 1
    if nsteps == 1:
        guarded_step(0, 0, 1)
    else:
        lax.fori_loop(0, nsteps // 2, body, 0, unroll=unroll)
    last = nsteps % 2
    o_t = acc_ref[last, 0:LANES, :] / acc_ref[last, LANES:LANES + 1, :]
    out = None
    for h in range(H):
        o_h = o_t[:, h * tq:(h + 1) * tq].T.astype(BF16)
        term = jnp.dot(o_h, e_ref[h], preferred_element_type=F32)
        out = term if out is None else out + term
    o_ref[0] = out.astype(BF16)


def _flash(q, k, vt, esel, *, tq, tk, cw=256, unroll=1):
    B, P, H, S, dk = q.shape
    W = esel.shape[2]
    tq = _tile(S, tq)
    tk = _tile(S, tk)
    R = H * tq
    cw = _tile(R, cw)
    return pl.pallas_call(
        functools.partial(_flash_kernel, tk=tk, cw=cw, unroll=unroll),
        out_shape=jax.ShapeDtypeStruct((B, S, P * W), BF16),
        grid=(B, P, S // tq),
        in_specs=[pl.BlockSpec((1, 1, H, tq, dk), lambda b, p, i: (b, p, 0, i, 0)),
                  pl.BlockSpec((1, 1, S, dk), lambda b, p, i: (b, p, 0, 0)),
                  pl.BlockSpec((1, 1, VT_ROWS, S), lambda b, p, i: (b, p, 0, 0)),
                  pl.BlockSpec((H, LANES, W), lambda b, p, i: (0, 0, 0))],
        out_specs=pl.BlockSpec((1, tq, W), lambda b, p, i: (b, i, p)),
        scratch_shapes=[pltpu.VMEM((dk, R), BF16), pltpu.VMEM((2, 1, R), F32), pltpu.VMEM((2, VT_ROWS, R), F32)],
        compiler_params=_cparams(("arbitrary", "arbitrary", "arbitrary")),
        name="flash_attn",
    )(q, k, vt, esel)


def _window_kernel(q_ref, k_ref, vt_ref, e_ref, slope_ref, sink_ref, o_ref, *, tq):
    H = q_ref.shape[1]
    S = k_ref.shape[1]
    wk = min(S, tq + 2 * WINDOW)
    i = pl.program_id(1)
    start = pl.multiple_of(jnp.clip(i * tq - WINDOW, 0, S - wk), WINDOW)
    k = k_ref[0, pl.ds(start, wk), :]
    vt = vt_ref[0, :, pl.ds(start, wk)]
    spos = start + lax.broadcasted_iota(jnp.int32, (wk, tq), 0)
    tpos = i * tq + lax.broadcasted_iota(jnp.int32, (wk, tq), 1)
    dist = jnp.abs(spos - tpos)
    inside = dist <= WINDOW
    distf = dist.astype(F32)
    qt = jnp.concatenate([q_ref[0, h].astype(F32).T.astype(BF16) for h in range(H)], axis=1)
    s_all = jnp.dot(k, qt, preferred_element_type=F32)
    es, ms = [], []
    for h in range(H):
        s = s_all[:, h * tq:(h + 1) * tq] - slope_ref[h] * distf
        s = jnp.where(inside, s, NEG_INF)
        m = jnp.maximum(jnp.max(s, axis=0, keepdims=True), sink_ref[h])
        es.append(jnp.exp2(s - m).astype(BF16))
        ms.append(m)
    acc = jnp.dot(vt, jnp.concatenate(es, axis=1), preferred_element_type=F32)
    out = None
    for h in range(H):
        sl = slice(h * tq, (h + 1) * tq)
        den = acc[LANES:LANES + 1, sl] + jnp.exp2(sink_ref[h] - ms[h])
        o = (acc[0:LANES, sl] / den).T.astype(BF16)
        term = jnp.dot(o, e_ref[h], preferred_element_type=F32)
        out = term if out is None else out + term
    o_ref[0] = out.astype(BF16)


def _window(q, k, v, esel, slopes, sink, *, tq):
    B, H, S, _ = q.shape
    W = esel.shape[2]
    tq = _tile(S, tq)
    return pl.pallas_call(
        functools.partial(_window_kernel, tq=tq),
        out_shape=jax.ShapeDtypeStruct((B, S, W), BF16),
        grid=(B, S // tq),
        in_specs=[pl.BlockSpec((1, H, tq, LANES), lambda b, i: (b, 0, i, 0)),
                  pl.BlockSpec((1, S, LANES), lambda b, i: (b, 0, 0)),
                  pl.BlockSpec((1, VT_ROWS, S), lambda b, i: (b, 0, 0)),
                  pl.BlockSpec((H, LANES, W), lambda b, i: (0, 0, 0)),
                  pl.BlockSpec((H, 1, 1), lambda b, i: (0, 0, 0)),
                  pl.BlockSpec((H, 1, 1), lambda b, i: (0, 0, 0))],
        out_specs=pl.BlockSpec((1, tq, W), lambda b, i: (b, i, 0)),
        compiler_params=_cparams(("arbitrary", "arbitrary")),
        name="window_attn",
    )(q, k, v, esel, slopes, sink)


def _sigmoid(v):
    return 1.0 / (1.0 + jnp.exp(-v))


def _pack_token_words(x):
    rp = x.shape[1] // (2 * LANES)
    words = []
    for r in range(rp):
        lo = lax.bitcast_convert_type(x[:, LANES * r:LANES * (r + 1)].astype(BF16).astype(F32), jnp.uint32)
        hi = lax.bitcast_convert_type(x[:, LANES * (rp + r):LANES * (rp + r + 1)].astype(BF16).astype(F32), jnp.uint32)
        words.append((lo >> 16) | hi)
    return words


def _load_token_words(ref, row0, n, rp):
    return [ref[pl.ds(row0 + r, n, stride=rp), :] for r in range(rp)]


def _unpack_token_words(words):
    lo = [lax.bitcast_convert_type(w << 16, F32) for w in words]
    hi = [lax.bitcast_convert_type(w & jnp.uint32(0xFFFF0000), F32) for w in words]
    return jnp.concatenate(lo + hi, axis=1)


def _post_kernel(oa_ref, ob_ref, oc_ref, x_ref, g1_ref, sc2_ref, sh2_ref, g2_ref, og_ref, wout_ref, n2_ref, rwt_ref,
                 sw1_ref, sw3_ref, sw2_ref, xs1_ref, u2_ref, st_ref):
    og = og_ref[...]

    def rn(o_ref, g):
        o = o_ref[0].astype(F32)
        return (o * lax.rsqrt(jnp.mean(o * o, axis=-1, keepdims=True) + EPS) * g).astype(BF16)

    mixed = jnp.concatenate([rn(oa_ref, og[:, :A_WIDTH]), rn(ob_ref, og[:, A_WIDTH:A_WIDTH + B_WIDTH]),
                             rn(oc_ref, og[:, A_WIDTH + B_WIDTH:])], axis=-1)
    x1 = x_ref[0] + g1_ref[0] * jnp.dot(mixed, wout_ref[...], preferred_element_type=F32)
    u2 = x1 * lax.rsqrt(jnp.mean(x1 * x1, axis=-1, keepdims=True) + EPS) * n2_ref[...]
    u2 = u2 * (1.0 + sc2_ref[0]) + sh2_ref[0]
    words = _pack_token_words(u2)
    for r, w in enumerate(words):
        u2_ref[pl.ds(r, u2.shape[0], stride=len(words)), :] = w
    ub = u2.astype(BF16)
    logits_t = lax.dot_general(rwt_ref[...], ub, (((1,), (1,)), ((), ())), preferred_element_type=F32)
    st_ref[...] = _sigmoid(logits_t)
    h1 = jnp.dot(ub, sw1_ref[...], preferred_element_type=F32)
    h3 = jnp.dot(ub, sw3_ref[...], preferred_element_type=F32)
    hid = (h1 * _sigmoid(h1) * h3).astype(BF16)
    xs1_ref[0] = x1 + g2_ref[0] * jnp.dot(hid, sw2_ref[...], preferred_element_type=F32)


def _post(oa, ob, oc, x, g1, sc2, sh2, g2, og, wout, n2, rwt, sw1, sw3, sw2):
    B, S, D = x.shape
    E = rwt.shape[0]
    tm = _tile(S, 512)
    nt = S // tm
    full = lambda a: pl.BlockSpec(a.shape, lambda b, i: (0,) * a.ndim)
    mod = pl.BlockSpec((1, 1, D), lambda b, i: (b, 0, 0))
    row = lambda w: pl.BlockSpec((1, tm, w), lambda b, i: (b, i, 0))
    return pl.pallas_call(
        _post_kernel,
        out_shape=(jax.ShapeDtypeStruct((B, S, D), F32),
                   jax.ShapeDtypeStruct((B * S * D // (2 * LANES), LANES), jnp.uint32),
                   jax.ShapeDtypeStruct((E, B * S), F32)),
        grid=(B, nt),
        in_specs=[row(A_WIDTH), row(B_WIDTH), row(C_WIDTH), row(D), mod, mod, mod, mod, full(og), full(wout), full(n2),
                  full(rwt), full(sw1), full(sw3), full(sw2)],
        out_specs=(row(D), pl.BlockSpec((tm * D // (2 * LANES), LANES), lambda b, i: (b * nt + i, 0)),
                   pl.BlockSpec((E, tm), lambda b, i: (0, b * nt + i))),
        compiler_params=_cparams(("arbitrary", "arbitrary")),
        name="post_attn",
    )(oa, ob, oc, x, g1, sc2, sh2, g2, og, wout, n2, rwt, sw1, sw3, sw2)


def _route_kernel(st_ref, rb_ref, tri_ref, idx_ref, w_ref, rank_ref, cnt_ref, carry_ref):
    E, tm = st_ref.shape
    per = E // N_GROUPS
    big = float(E)

    @pl.when(pl.program_id(0) == 0)
    def _():
        carry_ref[...] = jnp.zeros(carry_ref.shape, F32)

    s = st_ref[...]
    choice = s + rb_ref[...]
    c3 = choice.reshape(N_GROUPS, per, tm)
    io3 = lax.broadcasted_iota(jnp.int32, c3.shape, 1).astype(F32)
    m1 = jnp.max(c3, axis=1, keepdims=True)
    i1 = jnp.min(jnp.where(c3 == m1, io3, big), axis=1, keepdims=True)
    m2 = jnp.max(jnp.where(io3 == i1, -jnp.inf, c3), axis=1, keepdims=True)
    gs = (m1 + m2).reshape(N_GROUPS, tm)
    gio = lax.broadcasted_iota(jnp.int32, gs.shape, 0).astype(F32)
    keep = jnp.zeros(gs.shape, F32)
    for _ in range(TOPK_GROUPS):
        mg = jnp.max(gs, axis=0, keepdims=True)
        ig = jnp.min(jnp.where(gs == mg, gio, big), axis=0, keepdims=True)
        sel = gio == ig
        keep = jnp.where(sel, 1.0, keep)
        gs = jnp.where(sel, -jnp.inf, gs)
    cur = jnp.where(keep.reshape(N_GROUPS, 1, tm) > 0.5, c3, NEG_INF).reshape(E, tm)
    eio = lax.broadcasted_iota(jnp.int32, (E, tm), 0).astype(F32)
    sels, idxs, ws = [], [], []
    for _ in range(TOP_K):
        m = jnp.max(cur, axis=0, keepdims=True)
        ik = jnp.min(jnp.where(cur == m, eio, big), axis=0, keepdims=True)
        sel = eio == ik
        ws.append(jnp.sum(jnp.where(sel, s, 0.0), axis=0, keepdims=True))
        cur = jnp.where(sel, -jnp.inf, cur)
        sels.append(sel)
        idxs.append(ik)
    wsum = ws[0]
    for wk in ws[1:]:
        wsum = wsum + wk
    w_ref[...] = jnp.concatenate([wk / wsum * ROUTED_SCALE for wk in ws], axis=0)
    idx_ref[...] = jnp.concatenate(idxs, axis=0).astype(jnp.int32)
    onehot = jnp.zeros((E, tm), F32)
    for sel in sels:
        onehot = jnp.where(sel, 1.0, onehot)
    prefix = jnp.dot(onehot.astype(BF16), tri_ref[...], preferred_element_type=F32) + carry_ref[...]
    rank_ref[...] = jnp.concatenate(
        [jnp.sum(jnp.where(sel, prefix, 0.0), axis=0, keepdims=True) for sel in sels], axis=0).astype(jnp.int32)
    carry = carry_ref[...] + jnp.sum(onehot, axis=1, keepdims=True)
    carry_ref[...] = carry
    cnt_ref[...] = jnp.broadcast_to(carry, cnt_ref.shape)


def _route(scores_t, router_b):
    E, N = scores_t.shape
    tm = _tile(N, 512)
    tri = (jnp.arange(tm)[:, None] < jnp.arange(tm)[None, :]).astype(BF16)
    return pl.pallas_call(
        _route_kernel,
        out_shape=(jax.ShapeDtypeStruct((TOP_K, N), jnp.int32), jax.ShapeDtypeStruct((TOP_K, N), F32),
                   jax.ShapeDtypeStruct((TOP_K, N), jnp.int32), jax.ShapeDtypeStruct((E, LANES), F32)),
        grid=(N // tm,),
        in_specs=[pl.BlockSpec((E, tm), lambda i: (0, i)), pl.BlockSpec((E, 1), lambda i: (0, 0)),
                  pl.BlockSpec((tm, tm), lambda i: (0, 0))],
        out_specs=(pl.BlockSpec((TOP_K, tm), lambda i: (0, i)), pl.BlockSpec((TOP_K, tm), lambda i: (0, i)),
                   pl.BlockSpec((TOP_K, tm), lambda i: (0, i)), pl.BlockSpec((E, LANES), lambda i: (0, 0))),
        scratch_shapes=[pltpu.VMEM((E, 1), F32)],
        compiler_params=_cparams(("arbitrary",)),
        name="route_topk",
    )(scores_t, router_b.reshape(E, 1), tri)


def _slots_kernel(idx_ref, rank_ref, starts_ref, dest_ref):
    E = starts_ref.shape[0]
    tm = idx_ref.shape[1]
    eio = lax.broadcasted_iota(jnp.int32, (E, tm), 0)
    starts = starts_ref[...]
    rows = [jnp.sum(jnp.where(eio == idx_ref[k:k + 1, :], starts, 0.0), axis=0, keepdims=True) for k in range(TOP_K)]
    dest_ref[...] = jnp.concatenate(rows, axis=0).astype(jnp.int32) + rank_ref[...]


def _slots(idx, rank, starts):
    K, N = idx.shape
    E = starts.shape[0]
    tm = _tile(N, 512)
    blk = pl.BlockSpec((K, tm), lambda i: (0, i))
    return pl.pallas_call(
        _slots_kernel, out_shape=jax.ShapeDtypeStruct((K, N), jnp.int32), grid=(N // tm,),
        in_specs=[blk, blk, pl.BlockSpec((E, 1), lambda i: (0, 0))], out_specs=blk,
        compiler_params=_cparams(("arbitrary",)), name="moe_slots",
    )(idx, rank, starts.astype(F32).reshape(E, 1))


TOKENS_PER_ISSUE = 8


def _dispatch_kernel(dest_ref, u_ref, xs_ref, sem, *, ch):
    tm = u_ref.shape[0] // ch

    def body(g, carry):
        base = pl.multiple_of(g * (TOKENS_PER_ISSUE * ch), TOKENS_PER_ISSUE * ch)
        for r in range(TOKENS_PER_ISSUE):
            for k in range(TOP_K):
                d = dest_ref[g * (TOKENS_PER_ISSUE * TOP_K) + r * TOP_K + k]
                pltpu.make_async_copy(u_ref.at[pl.ds(base + r * ch, ch)],
                                      xs_ref.at[pl.ds(pl.multiple_of(d * ch, ch), ch)],
                                      sem).start(priority=k % 2)
        return carry

    lax.fori_loop(0, tm // TOKENS_PER_ISSUE, body, 0)
    for k in range(TOP_K):
        pltpu.make_async_copy(u_ref, xs_ref.at[pl.ds(0, tm * ch)], sem).wait()


def _dispatch(dest_flat, u2t, n_slots, ch):
    N = u2t.shape[0] // ch
    tm = _tile(N, 1024)
    return pl.pallas_call(
        functools.partial(_dispatch_kernel, ch=ch),
        out_shape=jax.ShapeDtypeStruct((n_slots * ch, LANES), u2t.dtype),
        grid=(N // tm,),
        in_specs=[pl.BlockSpec((tm * TOP_K,), lambda i: (i,), memory_space=pltpu.SMEM),
                  pl.BlockSpec((tm * ch, LANES), lambda i: (i, 0))],
        out_specs=pl.BlockSpec(memory_space=pl.ANY),
        scratch_shapes=[pltpu.SemaphoreType.DMA(())],
        compiler_params=pltpu.CompilerParams(dimension_semantics=("arbitrary",), vmem_limit_bytes=VMEM_LIMIT,
                                             has_side_effects=True),
        name="moe_dispatch",
    )(dest_flat, u2t)


def _expert_kernel(wb_ref, we_ref, lo_ref, hi_ref, nw_ref, xs_ref, w1_ref, w3_ref, w2_ref, ys_ref, w1b, w3b, w2b):
    j = pl.program_id(0)

    @pl.when(j < nw_ref[0])
    def _():
        prev = jnp.maximum(j - 1, 0)
        new_expert = jnp.logical_or(j == 0, we_ref[j] != we_ref[prev])
        new_block = jnp.logical_or(j == 0, wb_ref[j] != wb_ref[prev])

        @pl.when(new_expert)
        def _():
            w1b[...] = w1_ref[0].astype(BF16)
            w3b[...] = w3_ref[0].astype(BF16)
            w2b[...] = w2_ref[0].astype(BF16)

        rp = w1b.shape[0] // (2 * LANES)
        rows = lax.broadcasted_iota(jnp.int32, (MOE_BM, LANES), 0)
        mine = (rows >= lo_ref[j]) & (rows < hi_ref[j])
        words = [[] for _ in range(rp)]
        for s in range(MOE_SPLIT):
            r0, nr = s * (MOE_BM // MOE_SPLIT), MOE_BM // MOE_SPLIT
            xw = [jnp.where(mine[r0:r0 + nr], w, jnp.uint32(0)) for w in _load_token_words(xs_ref, r0 * rp, nr, rp)]
            xb = _unpack_token_words(xw).astype(BF16)
            h1 = jnp.dot(xb, w1b[...], preferred_element_type=F32)
            h3 = jnp.dot(xb, w3b[...], preferred_element_type=F32)
            hid = (h1 * _sigmoid(h1) * h3).astype(BF16)
            for r, w in enumerate(_pack_token_words(jnp.dot(hid, w2b[...], preferred_element_type=F32))):
                words[r].append(w)
        words = [jnp.concatenate(ws, axis=0) for ws in words]

        @pl.when(new_block)
        def _():
            for r, w in enumerate(words):
                ys_ref[pl.ds(r, MOE_BM, stride=rp), :] = w

        @pl.when(jnp.logical_not(new_block))
        def _():
            for r, prev in enumerate(_load_token_words(ys_ref, 0, MOE_BM, rp)):
                ys_ref[pl.ds(r, MOE_BM, stride=rp), :] = jnp.where(mine, words[r], prev)


def _experts(wb, we, lo, hi, nw, xs, w1, w3, w2, *, layer):
    D, F = w1.shape[2:]
    rows = MOE_BM * D // (2 * LANES)
    blk = lambda j, wb, we, lo, hi, nw: (wb[j], 0)
    wsel = lambda j, wb, we, lo, hi, nw: (layer, we[j], 0, 0)
    grid_spec = pltpu.PrefetchScalarGridSpec(
        num_scalar_prefetch=5, grid=(wb.shape[0],),
        in_specs=[pl.BlockSpec((rows, LANES), blk), pl.BlockSpec((None, 1, D, F), wsel),
                  pl.BlockSpec((None, 1, D, F), wsel), pl.BlockSpec((None, 1, F, D), wsel)],
        out_specs=pl.BlockSpec((rows, LANES), blk),
        scratch_shapes=[pltpu.VMEM((D, F), BF16), pltpu.VMEM((D, F), BF16), pltpu.VMEM((F, D), BF16)])
    return pl.pallas_call(
        _expert_kernel, out_shape=jax.ShapeDtypeStruct(xs.shape, xs.dtype), grid_spec=grid_spec,
        compiler_params=_cparams(("arbitrary",)), name="moe_experts",
    )(wb, we, lo, hi, nw, xs, w1, w3, w2)


def _combine_kernel(dcur_ref, dnext_ref, w_ref, xs1_ref, g2_ref, fg_ref, ys_ref, o_ref, buf, sems, *, final):
    tm, D = xs1_ref.shape
    ch = D // (2 * LANES)
    i = pl.program_id(0)
    n = pl.num_programs(0)

    def issue(dref, slot):
        def body(g, carry):
            base = pl.multiple_of(g * (TOKENS_PER_ISSUE * ch), TOKENS_PER_ISSUE * ch)
            for r in range(TOKENS_PER_ISSUE):
                for k in range(TOP_K):
                    d = dref[g * (TOKENS_PER_ISSUE * TOP_K) + r * TOP_K + k]
                    pltpu.make_async_copy(ys_ref.at[pl.ds(pl.multiple_of(d * ch, ch), ch)],
                                          buf.at[slot, k, pl.ds(base + r * ch, ch)],
                                          sems.at[slot]).start(priority=k % 2)
            return carry

        lax.fori_loop(0, tm // TOKENS_PER_ISSUE, body, 0)

    def finish(slot):
        for k in range(TOP_K):
            pltpu.make_async_copy(ys_ref.at[pl.ds(0, tm * ch)], buf.at[slot, k], sems.at[slot]).wait()
        w = w_ref[...]
        r = None
        for k in range(TOP_K):
            term = w[:, k:k + 1] * _unpack_token_words(_load_token_words(buf.at[slot, k], 0, tm, ch))
            r = term if r is None else r + term
        x2 = xs1_ref[...] + g2_ref[0] * r
        if final:
            x2 = x2 * lax.rsqrt(jnp.mean(x2 * x2, axis=-1, keepdims=True) + EPS) * fg_ref[...]
        o_ref[...] = x2

    @pl.when(i == 0)
    def _():
        issue(dcur_ref, 0)

    for slot in range(2):
        @pl.when(jnp.logical_and(i + 1 < n, (i + 1) % 2 == slot))
        def _():
            issue(dnext_ref, slot)

    for slot in range(2):
        @pl.when(i % 2 == slot)
        def _():
            finish(slot)


def _combine(dest_flat, w_tok, xs1, g2, fg, ys, *, seq, final):
    N, D = xs1.shape
    ch = D // (2 * LANES)
    tm = _tile(seq, 128)
    per_b = seq // tm
    n = N // tm
    return pl.pallas_call(
        functools.partial(_combine_kernel, final=final),
        out_shape=jax.ShapeDtypeStruct((N, D), F32),
        grid=(n,),
        in_specs=[pl.BlockSpec((tm * TOP_K,), lambda i: (i,), memory_space=pltpu.SMEM),
                  pl.BlockSpec((tm * TOP_K,), lambda i: (jnp.minimum(i + 1, n - 1),), memory_space=pltpu.SMEM),
                  pl.BlockSpec((tm, TOP_K), lambda i: (i, 0)),
                  pl.BlockSpec((tm, D), lambda i: (i, 0)),
                  pl.BlockSpec((1, 1, D), lambda i: (i // per_b, 0, 0)),
                  pl.BlockSpec((1, D), lambda i: (0, 0)),
                  pl.BlockSpec(memory_space=pl.ANY)],
        out_specs=pl.BlockSpec((tm, D), lambda i: (i, 0)),
        scratch_shapes=[pltpu.VMEM((2, TOP_K, tm * ch, LANES), jnp.uint32), pltpu.SemaphoreType.DMA((2,))],
        compiler_params=_cparams(("arbitrary",)),
        name="moe_combine",
    )(dest_flat, dest_flat, w_tok, xs1, g2, fg, ys)


def _rope_tables(S):
    half = HEAD_DIM // 2
    inv = ROPE_THETA ** (-jnp.arange(0, half, 2, dtype=F32) / half)
    t = jnp.arange(S, dtype=jnp.int32)

    def part(pos):
        ang = pos.astype(F32)[:, None] * inv[None, :]
        c, s, z = jnp.cos(ang), jnp.sin(ang), jnp.zeros_like(ang)
        return jnp.concatenate([c, c], -1), jnp.concatenate([-s, z], -1), jnp.concatenate([z, s], -1)

    row, col, lin = part(t // GRID_W), part(t % GRID_W), part(t)
    tabs_a = tuple(jnp.tile(jnp.concatenate([r, c], -1), (1, LANES // HEAD_DIM)) for r, c in zip(row, col))
    tabs_c = tuple(jnp.pad(v, ((0, 0), (0, LANES - C_ROPE))) for v in lin)
    return tabs_a, tabs_c


def _head_slabs(w, heads, groups):
    d = w.shape[0]
    hot = (jnp.arange(heads)[:, None] // (heads // groups) == jnp.arange(groups)[None, :]).astype(w.dtype)
    return (w.reshape(d, heads, 1, HEAD_DIM) * hot[None, :, :, None]).reshape(d, heads * LANES)


def _pad_w_in(w):
    d = w.shape[0]
    a, b, c = w[:, :A_IN], w[:, A_IN:A_IN + B_IN], w[:, A_IN + B_IN:]
    kr = jnp.pad(c[:, C_Q_LORA + C_KV_LORA:], ((0, 0), (0, LANES - C_ROPE)))
    return jnp.concatenate([_head_slabs(a[:, :A_WIDTH], A_HEADS, A_KV_HEADS), a[:, A_WIDTH:],
                            _head_slabs(b[:, :B_WIDTH], B_HEADS, B_KV_HEADS), b[:, B_WIDTH:],
                            c[:, :C_Q_LORA + C_KV_LORA], kr], axis=1).astype(BF16)


def _pad_w_uq(w):
    r = w.shape[0]
    w = w.reshape(r, C_HEADS, C_NOPE + C_ROPE)
    hot = (jnp.arange(C_HEADS)[:, None] % 2 == jnp.arange(2)[None, :]).astype(w.dtype)
    nope = (w[:, :, None, :C_NOPE] * hot[None, :, :, None]).reshape(r, C_HEADS, LANES)
    rope = jnp.pad(w[:, :, C_NOPE:], ((0, 0), (0, 0), (0, LANES - C_ROPE)))
    return jnp.concatenate([nope, rope], axis=-1).reshape(r, C_HEADS * 2 * LANES).astype(BF16)


def _select_mats(heads, groups, out_w):
    h = jnp.arange(heads)[:, None, None]
    r = jnp.arange(LANES)[None, :, None]
    c = jnp.arange(out_w)[None, None, :]
    half = h // (heads // groups)
    return ((r // HEAD_DIM == half) & (c == HEAD_DIM * h + r % HEAD_DIM)).astype(BF16)


def kernel(x, c, mod_w, mod_b, norm1_g, norm2_g, w_in, a_qnorm_g, a_knorm_g, b_sink, c_qnorm_g, c_kvnorm_g, c_w_uq, c_w_uk, c_w_uv, out_norm_g, w_out, router_w, router_b, exp_w1, exp_w3, exp_w2, sh_w1, sh_w3, sh_w2, final_g):
    B, S, D = x.shape
    L = mod_w.shape[0]
    E = router_w.shape[2]
    N = B * S
    n_slots = N * TOP_K
    assert n_slots % MOE_BM == 0
    n_work = n_slots // MOE_BM + E

    mod = _mod(c, mod_w, mod_b).reshape(L, B, 6, 1, D)
    tabs_a, tabs_c = _rope_tables(S)
    esel_a = _select_mats(A_HEADS, A_KV_HEADS, A_WIDTH)
    esel_c = _select_mats(2, 2, LANES)
    slopes = (jnp.exp2(-8.0 * jnp.arange(1, B_HEADS + 1, dtype=F32) / B_HEADS) * LOG2E).reshape(B_HEADS, 1, 1)

    for l in range(L):
        sh1, sc1, g1, sh2, sc2, g2 = [mod[l, :, i] for i in range(6)]
        gq2 = jnp.tile(a_qnorm_g[l], LANES // HEAD_DIM).reshape(1, LANES)
        gk2 = jnp.tile(a_knorm_g[l], LANES // HEAD_DIM).reshape(1, LANES)
        qa, ka, va, qb, kb, vb, qc, kc, vc = _inproj(
            x, sc1, sh1, norm1_g[l].reshape(1, D), _pad_w_in(w_in[l]), tabs_a, tabs_c, gq2, gk2,
            c_qnorm_g[l].reshape(1, -1), c_kvnorm_g[l].reshape(1, -1), _pad_w_uq(c_w_uq[l]),
            c_w_uk[l].astype(BF16), c_w_uv[l].astype(BF16))
        oa = _flash(qa[:, None], ka[:, None], va[:, None], esel_a, tq=1024, tk=512, cw=512, unroll=1)
        ob = _window(qb, kb, vb, esel_a, slopes, (b_sink[l].astype(F32) * LOG2E).reshape(B_HEADS, 1, 1), tq=256)
        oc = _flash(qc.reshape(B, C_HEADS // 2, 2, S, 2 * LANES), kc, vc, esel_c, tq=2048, tk=512, cw=512, unroll=1)
        xs1, u2, scores_t = _post(
            oa, ob, oc, x, g1, sc2, sh2, g2, out_norm_g[l].reshape(1, -1), w_out[l].astype(BF16),
            norm2_g[l].reshape(1, D), router_w[l].T.astype(BF16), sh_w1[l].astype(BF16), sh_w3[l].astype(BF16),
            sh_w2[l].astype(BF16))
        idx, w_top, rank, cnt = _route(scores_t, router_b[l])
        counts = cnt[:, 0].astype(jnp.int32)
        ends = jnp.cumsum(counts)
        starts = ends - counts
        dest = _slots(idx, rank, starts).T.reshape(N * TOP_K)
        first_blk = starts // MOE_BM
        n_items = jnp.where(counts > 0, (ends - 1) // MOE_BM - first_blk + 1, 0)
        item_ends = jnp.cumsum(n_items)
        nw = item_ends[-1].reshape(1).astype(jnp.int32)
        wid = jnp.minimum(jnp.arange(n_work, dtype=jnp.int32), nw[0] - 1)
        we = jnp.minimum(jnp.sum(item_ends[None, :] <= wid[:, None], axis=1), E - 1).astype(jnp.int32)
        hot = we[:, None] == jnp.arange(E, dtype=jnp.int32)[None, :]
        take = lambda tbl: jnp.sum(jnp.where(hot, tbl[None, :], 0), axis=1).astype(jnp.int32)
        wb = take(first_blk) + wid - take(item_ends - n_items)
        lo = jnp.clip(take(starts) - wb * MOE_BM, 0, MOE_BM).astype(jnp.int32)
        hi = jnp.clip(take(ends) - wb * MOE_BM, 0, MOE_BM).astype(jnp.int32)
        xs = _dispatch(dest, u2, n_slots, D // (2 * LANES))
        ys = _experts(wb, we, lo, hi, nw, xs, exp_w1, exp_w3, exp_w2, layer=l)
        x = _combine(dest, w_top.T, xs1.reshape(N, D), g2, final_g.reshape(1, D), ys, seq=S,
                     final=(l == L - 1)).reshape(B, S, D)
    return x
```
